```python
import jax, jax.numpy as jnp
from jax import lax
import numpy as np

D_MODEL = 1024
BATCH = 8
SEQ = 2048
DEPTH = 2

ATTN_W = D_MODEL // 2
N_ATTN_HEADS = 8
HEAD_DIM = ATTN_W // N_ATTN_HEADS
POOL_W = D_MODEL - ATTN_W
POOL_WINDOWS = (2, 4, 8, 16)
POOL_C = POOL_W // len(POOL_WINDOWS)
IN_COLS = 3 * ATTN_W + N_ATTN_HEADS + POOL_W
D_FF = 2816
CONV_W = 3
Q_BLOCK = 128
EPS = 1e-6

kernel_name = "hybrid_fox_pool_convffn_adaln"


def _rmsnorm(x, g):
    x32 = x.astype(jnp.float32)
    y = x32 * lax.rsqrt(jnp.mean(x32 * x32, axis=-1, keepdims=True) + EPS)
    return (y * g.astype(jnp.float32)).astype(x.dtype)


def _fox_attention(q, k, v, log_f):
    B, S, H, Dh = q.shape
    nb = S // Q_BLOCK
    F = jnp.cumsum(log_f, axis=1)
    Fk = F.transpose(0, 2, 1)
    q32 = q.astype(jnp.float32) * (Dh ** -0.5)
    k32 = k.astype(jnp.float32)
    v32 = v.astype(jnp.float32)
    qb = q32.reshape(B, nb, Q_BLOCK, H, Dh).transpose(1, 0, 2, 3, 4)
    Fq = F.reshape(B, nb, Q_BLOCK, H).transpose(1, 0, 3, 2)
    kpos = jnp.arange(S)

    def one_block(args):
        i, qi, Fi = args
        s = jnp.einsum('bqhd,bkhd->bhqk', qi, k32)
        s = s + Fi[..., None] - Fk[:, :, None, :]
        qpos = i * Q_BLOCK + jnp.arange(Q_BLOCK)
        s = jnp.where(qpos[:, None] >= kpos[None, :], s, -jnp.inf)
        p = jax.nn.softmax(s, axis=-1)
        return jnp.einsum('bhqk,bkhd->bqhd', p, v32)

    o = lax.map(one_block, (jnp.arange(nb), qb, Fq))
    return o.transpose(1, 0, 2, 3, 4).reshape(B, S, H * Dh)


def _multiscale_pool(u, pool_w, pool_scale):
    B, S, _ = u.shape
    u32 = u.astype(jnp.float32)
    count = jnp.arange(1, S + 1, dtype=jnp.float32)
    outs = []
    for g, w in enumerate(POOL_WINDOWS):
        ug = u32[..., g * POOL_C:(g + 1) * POOL_C]
        csp = jnp.concatenate([jnp.zeros((B, 1, POOL_C), jnp.float32),
                               jnp.cumsum(ug, axis=1)], axis=1)
        lo = jnp.concatenate([jnp.zeros((B, w - 1, POOL_C), jnp.float32),
                              csp[:, :S - w + 1]], axis=1)
        mean = (csp[:, 1:] - lo) / jnp.minimum(count, w)[None, :, None]
        outs.append(jnp.einsum('bsc,cd->bsd', mean - ug, pool_w[g].astype(jnp.float32)))
    out = jnp.concatenate(outs, axis=-1) * pool_scale.astype(jnp.float32)
    return out.astype(u.dtype)


def _conv_ffn(h, w_up, conv_w, conv_b, w_down):
    S = h.shape[1]
    a = h @ w_up
    ap = jnp.pad(a, ((0, 0), (CONV_W - 1, 0), (0, 0)))
    conv = conv_b + ap[:, 0:S] * conv_w[0]
    for j in range(1, CONV_W):
        conv = conv + ap[:, j:j + S] * conv_w[j]
    gate, val = jnp.split(conv, 2, axis=-1)
    return (jax.nn.silu(gate) * val) @ w_down


def setup_inputs(seed: int = 0) -> dict:
    key = jax.random.key(seed)
    ks = jax.random.split(key, 18)
    f32 = jnp.float32
    nrm = lambda k, shape, s: jax.random.normal(k, shape, f32) * s
    return {
        "x": nrm(ks[0], (BATCH, SEQ, D_MODEL), 1.0),
        "c": nrm(ks[1], (BATCH, D_MODEL), 1.0),
        "mod_w": nrm(ks[2], (DEPTH, D_MODEL, 6 * D_MODEL), 0.5 * D_MODEL ** -0.5),
        "mod_b": nrm(ks[3], (DEPTH, 6 * D_MODEL), 0.02),
        "norm1_g": 1.0 + nrm(ks[4], (DEPTH, D_MODEL), 0.02),
        "norm2_g": 1.0 + nrm(ks[5], (DEPTH, D_MODEL), 0.02),
        "w_in": nrm(ks[6], (DEPTH, D_MODEL, IN_COLS), D_MODEL ** -0.5),
        "b_f": jax.random.uniform(ks[7], (DEPTH, N_ATTN_HEADS), f32, 1.0, 6.0),
        "pool_w": nrm(ks[8], (DEPTH, len(POOL_WINDOWS), POOL_C, POOL_C), POOL_C ** -0.5),
        "pool_scale": 1.0 + nrm(ks[9], (DEPTH, POOL_W), 0.1),
        "w_out": nrm(ks[10], (DEPTH, ATTN_W + POOL_W, D_MODEL), (ATTN_W + POOL_W) ** -0.5),
        "ffn_up": nrm(ks[11], (DEPTH, D_MODEL, 2 * D_FF), D_MODEL ** -0.5),
        "ffn_conv_w": nrm(ks[12], (DEPTH, CONV_W, 2 * D_FF), CONV_W ** -0.5),
        "ffn_conv_b": nrm(ks[13], (DEPTH, 2 * D_FF), 0.02),
        "ffn_down": nrm(ks[14], (DEPTH, D_FF, D_MODEL), D_FF ** -0.5),
        "final_g": 1.0 + nrm(ks[15], (D_MODEL,), 0.02),
    }


def reference(x, c, mod_w, mod_b, norm1_g, norm2_g, w_in, b_f, pool_w, pool_scale,
              w_out, ffn_up, ffn_conv_w, ffn_conv_b, ffn_down, final_g):
    B, S, _ = x.shape
    c_act = jax.nn.silu(c)
    splits = [ATTN_W, 2 * ATTN_W, 3 * ATTN_W, 3 * ATTN_W + N_ATTN_HEADS]
    for l in range(DEPTH):
        mod = (c_act @ mod_w[l] + mod_b[l])[:, None, :]
        sh1, sc1, g1, sh2, sc2, g2 = jnp.split(mod, 6, axis=-1)

        h = _rmsnorm(x, norm1_g[l]) * (1.0 + sc1) + sh1
        z = h @ w_in[l]
        q, k, v, f_logit, u = jnp.split(z, splits, axis=-1)
        log_f = jax.nn.log_sigmoid((f_logit + b_f[l]).astype(jnp.float32))
        hd = (B, S, N_ATTN_HEADS, HEAD_DIM)
        attn = _fox_attention(q.reshape(hd), k.reshape(hd), v.reshape(hd), log_f)
        pool = _multiscale_pool(u, pool_w[l], pool_scale[l])
        mixed = jnp.concatenate([attn.astype(x.dtype), pool], axis=-1)
        x = x + g1 * (mixed @ w_out[l])

        h = _rmsnorm(x, norm2_g[l]) * (1.0 + sc2) + sh2
        x = x + g2 * _conv_ffn(h, ffn_up[l], ffn_conv_w[l], ffn_conv_b[l], ffn_down[l])
    return _rmsnorm(x, final_g)
```

```python
import functools

import jax
import jax.numpy as jnp
from jax import lax
from jax.experimental import pallas as pl
from jax.experimental.pallas import tpu as pltpu

F32 = jnp.float32
BF16 = jnp.bfloat16

D_MODEL = 1024
N_HEADS = 8
HEAD_DIM = 64
ATTN_W = N_HEADS * HEAD_DIM
POOL_WINDOWS = (2, 4, 8, 16)
POOL_C = 128
POOL_W = POOL_C * len(POOL_WINDOWS)
D_FF = 2816
CONV_W = 3
EPS = 1e-6

LANES = 128
SUBLANES = 8
HALO = 16
FF_CHUNK = 256
N_FF_CHUNKS = D_FF // FF_CHUNK
F_PIECES = 3
ONES_LANE = 32
NEG_BIG = -1e30

TM_PREMIX = 512
T_ATTN = 256
TM_FFN = 256
TN_MOD = 1024
VMEM_LIMIT = 56 * 1024 * 1024


def _split3(x):
    hi = x.astype(BF16).astype(F32)
    r = x - hi
    mid = r.astype(BF16).astype(F32)
    lo = (r - mid).astype(BF16).astype(F32)
    return hi, mid, lo


def _mod_kernel(c_ref, w_ref, b_ref, o_ref):
    c = c_ref[...]
    ca = (c * jax.nn.sigmoid(c)).astype(BF16)
    o_ref[...] = jnp.dot(ca, w_ref[...].astype(BF16), preferred_element_type=F32) + b_ref[...]


def _modulation(c, mod_w, mod_b):
    depth, d, n = mod_w.shape
    b = c.shape[0]
    return pl.pallas_call(
        _mod_kernel,
        grid=(depth, n // TN_MOD),
        in_specs=[
            pl.BlockSpec((b, d), lambda l, j: (0, 0)),
            pl.BlockSpec((None, d, TN_MOD), lambda l, j: (l, 0, j)),
            pl.BlockSpec((None, 1, TN_MOD), lambda l, j: (l, 0, j)),
        ],
        out_specs=pl.BlockSpec((None, b, TN_MOD), lambda l, j: (l, 0, j)),
        out_shape=jax.ShapeDtypeStruct((depth, b, n), F32),
        compiler_params=pltpu.CompilerParams(
            dimension_semantics=("arbitrary", "arbitrary"), vmem_limit_bytes=VMEM_LIMIT),
        name="modulation",
    )(c, mod_w, mod_b.reshape(depth, 1, n))


def _premix_kernel(x_ref, mod_ref, g_ref, wku_ref, wt_ref, bf_ref, pw_ref, ps_ref, tri_ref,
                   k_ref, fk_ref, qt_ref, vt_ref, ft_ref, pool_ref,
                   carry_ref, ubuf_ref, *, tm):
    i = pl.program_id(1)

    @pl.when(i == 0)
    def _():
        carry_ref[...] = jnp.zeros_like(carry_ref)
        ubuf_ref[0:HALO, :] = jnp.zeros((HALO, POOL_W), F32)

    x = x_ref[...]
    ms = jnp.mean(x * x, axis=-1, keepdims=True)
    scale = g_ref[...] * (1.0 + mod_ref[1:2, :])
    h = (x * lax.rsqrt(ms + EPS)) * scale + mod_ref[0:1, :]
    hb = h.astype(BF16)

    ku = jnp.dot(hb, wku_ref[...], preferred_element_type=F32)
    k_ref[...] = ku[:, :ATTN_W].astype(BF16)
    u = ku[:, ATTN_W:]

    zt = lax.dot_general(wt_ref[...], hb, (((1,), (1,)), ((), ())),
                         preferred_element_type=F32)
    qt_ref[...] = zt[0:ATTN_W].astype(BF16)
    vt_ref[...] = zt[ATTN_W:2 * ATTN_W].astype(BF16)

    fl = zt[2 * ATTN_W:2 * ATTN_W + N_HEADS] + bf_ref[...]
    logf = jnp.minimum(fl, 0.0) - jnp.log1p(jnp.exp(-jnp.abs(fl)))
    pieces = jnp.concatenate(list(_split3(logf)) + [jnp.zeros((SUBLANES, tm), F32)], axis=0)
    cs = jnp.dot(pieces.astype(BF16), tri_ref[...], preferred_element_type=F32)
    f_cum = (cs[0:8] + cs[8:16] + cs[16:24]) + carry_ref[:, 0:1]
    carry_ref[...] = jnp.broadcast_to(f_cum[:, tm - 1:tm], carry_ref.shape)
    ft_ref[...] = f_cum

    blk = jnp.concatenate(list(_split3(-f_cum))
                          + [jnp.zeros((LANES - F_PIECES * N_HEADS, tm), F32)], axis=0)
    blk_t = blk.T
    lane = lax.broadcasted_iota(jnp.int32, (tm, LANES), 1)
    blk_t = jnp.where((lane >= ONES_LANE) & (lane < ONES_LANE + F_PIECES), 1.0, blk_t)
    fk_ref[...] = blk_t.astype(BF16)

    ubuf_ref[HALO:HALO + tm, :] = u
    pos = i * tm + lax.broadcasted_iota(jnp.int32, (tm, POOL_C), 0)
    for g, w in enumerate(POOL_WINDOWS):
        c0 = g * POOL_C
        ug = u[:, c0:c0 + POOL_C]
        ws = ug
        for j in range(1, w):
            ws = ws + ubuf_ref[HALO - j:HALO - j + tm, c0:c0 + POOL_C]
        cnt = jnp.minimum(pos + 1, w).astype(F32)
        d = ws / cnt - ug
        pg = jnp.dot(d.astype(BF16), pw_ref[g], preferred_element_type=F32)
        pool_ref[:, c0:c0 + POOL_C] = (pg * ps_ref[:, c0:c0 + POOL_C]).astype(BF16)
    ubuf_ref[0:HALO, :] = ubuf_ref[tm:tm + HALO, :]


def _premix(x, mod, g1, wku, wt, b_f, pool_w, pool_scale, tri):
    b, s, d = x.shape
    tm = TM_PREMIX
    nrow_t = wt.shape[0]
    kern = functools.partial(_premix_kernel, tm=tm)
    row = lambda bi, i: (bi, i, 0)
    col = lambda bi, i: (bi, 0, i)
    const2 = lambda bi, i: (0, 0)
    return pl.pallas_call(
        kern,
        grid=(b, s // tm),
        in_specs=[
            pl.BlockSpec((None, tm, d), row),
            pl.BlockSpec((None, 6, d), lambda bi, i: (bi, 0, 0)),
            pl.BlockSpec((1, d), const2),
            pl.BlockSpec((d, ATTN_W + POOL_W), const2),
            pl.BlockSpec((nrow_t, d), const2),
            pl.BlockSpec((N_HEADS, 1), const2),
            pl.BlockSpec((len(POOL_WINDOWS), POOL_C, POOL_C), lambda bi, i: (0, 0, 0)),
            pl.BlockSpec((1, POOL_W), const2),
            pl.BlockSpec((tm, tm), const2),
        ],
        out_specs=[
            pl.BlockSpec((None, tm, ATTN_W), row),
            pl.BlockSpec((None, tm, LANES), row),
            pl.BlockSpec((None, ATTN_W, tm), col),
            pl.BlockSpec((None, ATTN_W, tm), col),
            pl.BlockSpec((None, N_HEADS, tm), col),
            pl.BlockSpec((None, tm, POOL_W), row),
        ],
        out_shape=[
            jax.ShapeDtypeStruct((b, s, ATTN_W), BF16),
            jax.ShapeDtypeStruct((b, s, LANES), BF16),
            jax.ShapeDtypeStruct((b, ATTN_W, s), BF16),
            jax.ShapeDtypeStruct((b, ATTN_W, s), BF16),
            jax.ShapeDtypeStruct((b, N_HEADS, s), F32),
            jax.ShapeDtypeStruct((b, s, POOL_W), BF16),
        ],
        scratch_shapes=[
            pltpu.VMEM((N_HEADS, LANES), F32),
            pltpu.VMEM((HALO + tm, POOL_W), F32),
        ],
        compiler_params=pltpu.CompilerParams(
            dimension_semantics=("arbitrary", "arbitrary"), vmem_limit_bytes=VMEM_LIMIT),
        name="premix",
    )(x, mod, g1, wku, wt, b_f, pool_w, pool_scale, tri)


def _attn_kernel(k_ref, fk_ref, qt_ref, vt_ref, ft_ref, o_ref, *, t):
    p = pl.program_id(1)
    qi = pl.program_id(2)
    row = lax.broadcasted_iota(jnp.int32, (LANES, t), 0)
    qt2 = qt_ref[...]
    diag_mask = (lax.broadcasted_iota(jnp.int32, (t, t), 0)
                 <= lax.broadcasted_iota(jnp.int32, (t, t), 1))

    outs = []
    for hh in range(2):
        head = 2 * p + hh
        q_h = jnp.where((row >= hh * HEAD_DIM) & (row < (hh + 1) * HEAD_DIM), qt2, jnp.zeros_like(qt2))
        pieces = _split3(ft_ref[hh:hh + 1, :])
        fblk = jnp.where((row < F_PIECES * N_HEADS) & (row % N_HEADS == head), 1.0, 0.0)
        for j in range(F_PIECES):
            fblk = jnp.where(row == ONES_LANE + j, pieces[j], fblk)
        q_aug = jnp.concatenate([q_h, fblk.astype(BF16)], axis=0)
        vt_h = vt_ref.at[hh * HEAD_DIM:(hh + 1) * HEAD_DIM, :]

        def tile(r0, carry, masked):
            m, l, acc = carry
            k_aug = jnp.concatenate([k_ref[pl.ds(r0, t), :], fk_ref[pl.ds(r0, t), :]], axis=1)
            s = jnp.dot(k_aug, q_aug, preferred_element_type=F32)
            if masked:
                s = jnp.where(diag_mask, s, -jnp.inf)
            m_new = jnp.maximum(m, jnp.max(s, axis=0, keepdims=True))
            alpha = jnp.exp(m - m_new)
            pr = jnp.exp(s - m_new)
            l = alpha * l + jnp.sum(pr, axis=0, keepdims=True)
            pv = jnp.dot(vt_h[:, pl.ds(r0, t)], pr.astype(BF16), preferred_element_type=F32)
            return m_new, l, alpha * acc + pv

        init = (jnp.full((1, t), NEG_BIG, F32), jnp.zeros((1, t), F32),
                jnp.zeros((HEAD_DIM, t), F32))
        carry = lax.fori_loop(
            0, qi, lambda j, c: tile(pl.multiple_of(j * t, t), c, False), init)
        m, l, acc = tile(pl.multiple_of(qi * t, t), carry, True)
        outs.append(acc / l)
    o_ref[...] = jnp.concatenate(outs, axis=0).T.astype(BF16)


def _attention(k, fk, qt, vt, ft):
    b, s, _ = k.shape
    t = T_ATTN
    n_pairs = N_HEADS // 2
    kern = functools.partial(_attn_kernel, t=t)
    return pl.pallas_call(
        kern,
        grid=(b, n_pairs, s // t),
        in_specs=[
            pl.BlockSpec((None, s, LANES), lambda bi, p, qi: (bi, 0, p)),
            pl.BlockSpec((None, s, LANES), lambda bi, p, qi: (bi, 0, 0)),
            pl.BlockSpec((None, LANES, t), lambda bi, p, qi: (bi, p, qi)),
            pl.BlockSpec((None, LANES, s), lambda bi, p, qi: (bi, p, 0)),
            pl.BlockSpec((None, None, 2, t), lambda bi, p, qi: (bi, p, 0, qi)),
        ],
        out_specs=pl.BlockSpec((None, t, LANES), lambda bi, p, qi: (bi, qi, p)),
        out_shape=jax.ShapeDtypeStruct((b, s, ATTN_W), BF16),
        compiler_params=pltpu.CompilerParams(
            dimension_semantics=("arbitrary", "arbitrary", "arbitrary"),
            vmem_limit_bytes=VMEM_LIMIT),
        name="fox_attention",
    )(k, fk, qt, vt, ft.reshape(b, n_pairs, 2, s))


def _ffn_kernel(x_ref, attn_ref, pool_ref, mod_ref, g_ref, wo_ref, wup_ref, cw_ref, cb_ref,
                wdn_ref, fg_ref, o_ref, h_ref, act_ref, abuf_ref, carry_ref, *, tm, final):
    i = pl.program_id(1)

    @pl.when(i == 0)
    def _():
        carry_ref[...] = jnp.zeros_like(carry_ref)

    mixed = jnp.concatenate([attn_ref[...], pool_ref[...]], axis=1)
    y = jnp.dot(mixed, wo_ref[...], preferred_element_type=F32)
    x1 = x_ref[...] + mod_ref[2:3, :] * y
    o_ref[...] = x1
    ms = jnp.mean(x1 * x1, axis=-1, keepdims=True)
    scale = g_ref[...] * (1.0 + mod_ref[4:5, :])
    h_ref[...] = ((x1 * lax.rsqrt(ms + EPS)) * scale + mod_ref[3:4, :]).astype(BF16)

    cw = 2 * FF_CHUNK
    for c in range(N_FF_CHUNKS):
        cols = slice(c * cw, (c + 1) * cw)
        a = jnp.dot(h_ref[...], wup_ref[:, cols], preferred_element_type=F32)
        abuf_ref[0:SUBLANES, :] = carry_ref[:, cols]
        abuf_ref[SUBLANES:SUBLANES + tm, :] = a
        carry_ref[:, cols] = a[tm - SUBLANES:tm, :]
        conv = (cb_ref[:, cols]
                + abuf_ref[SUBLANES - 2:SUBLANES - 2 + tm, :] * cw_ref[0:1, cols]
                + abuf_ref[SUBLANES - 1:SUBLANES - 1 + tm, :] * cw_ref[1:2, cols]
                + a * cw_ref[2:3, cols])
        gate = conv[:, :FF_CHUNK]
        val = conv[:, FF_CHUNK:]
        act_ref[:, c * FF_CHUNK:(c + 1) * FF_CHUNK] = (gate * jax.nn.sigmoid(gate) * val).astype(BF16)

    y2 = jnp.dot(act_ref[...], wdn_ref[...], preferred_element_type=F32)
    x2 = o_ref[...] + mod_ref[5:6, :] * y2
    if final:
        ms2 = jnp.mean(x2 * x2, axis=-1, keepdims=True)
        x2 = (x2 * lax.rsqrt(ms2 + EPS)) * fg_ref[...]
    o_ref[...] = x2


def _ffn(x, attn, pool, mod, g2, wo, wup, cw, cb, wdn, fg, final):
    b, s, d = x.shape
    tm = TM_FFN
    kern = functools.partial(_ffn_kernel, tm=tm, final=final)
    row = lambda bi, i: (bi, i, 0)
    const2 = lambda bi, i: (0, 0)
    once = pl.Buffered(1)
    return pl.pallas_call(
        kern,
        grid=(b, s // tm),
        in_specs=[
            pl.BlockSpec((None, tm, d), row),
            pl.BlockSpec((None, tm, ATTN_W), row),
            pl.BlockSpec((None, tm, POOL_W), row),
            pl.BlockSpec((None, 6, d), lambda bi, i: (bi, 0, 0)),
            pl.BlockSpec((1, d), const2),
            pl.BlockSpec((d, d), const2, pipeline_mode=once),
            pl.BlockSpec((d, 2 * D_FF), const2, pipeline_mode=once),
            pl.BlockSpec((CONV_W, 2 * D_FF), const2),
            pl.BlockSpec((1, 2 * D_FF), const2),
            pl.BlockSpec((D_FF, d), const2, pipeline_mode=once),
            pl.BlockSpec((1, d), const2),
        ],
        out_specs=pl.BlockSpec((None, tm, d), row),
        out_shape=jax.ShapeDtypeStruct((b, s, d), F32),
        scratch_shapes=[
            pltpu.VMEM((tm, d), BF16),
            pltpu.VMEM((tm, D_FF), BF16),
            pltpu.VMEM((SUBLANES + tm, 2 * FF_CHUNK), F32),
            pltpu.VMEM((SUBLANES, 2 * D_FF), F32),
        ],
        compiler_params=pltpu.CompilerParams(
            dimension_semantics=("arbitrary", "arbitrary"), vmem_limit_bytes=VMEM_LIMIT),
        name="outproj_convffn",
    )(x, attn, pool, mod, g2, wo, wup, cw, cb, wdn, fg)


def _chunk_gate_val(a):
    lead = a.shape[:-1]
    a = a.reshape(lead + (2, N_FF_CHUNKS, FF_CHUNK))
    a = jnp.swapaxes(a, -3, -2)
    return a.reshape(lead + (2 * D_FF,))


def kernel(x, c, mod_w, mod_b, norm1_g, norm2_g, w_in, b_f, pool_w, pool_scale, w_out, ffn_up,
           ffn_conv_w, ffn_conv_b, ffn_down, final_g):
    b, s, d = x.shape
    depth = mod_w.shape[0]
    assert (d, s % TM_PREMIX, s % T_ATTN, s % TM_FFN) == (D_MODEL, 0, 0, 0)

    mod = _modulation(c, mod_w, mod_b).reshape(depth, b, 6, d)
    tri = (lax.broadcasted_iota(jnp.int32, (TM_PREMIX, TM_PREMIX), 0)
           <= lax.broadcasted_iota(jnp.int32, (TM_PREMIX, TM_PREMIX), 1)).astype(BF16)
    a0, a1, a2, a3 = ATTN_W, 2 * ATTN_W, 3 * ATTN_W, 3 * ATTN_W + N_HEADS
    fg = final_g.reshape(1, d)

    for l in range(depth):
        wi = w_in[l]
        wku = jnp.concatenate([wi[:, a0:a1], wi[:, a3:]], axis=1).astype(BF16)
        wt = jnp.concatenate(
            [wi[:, :a0].T * (HEAD_DIM ** -0.5), wi[:, a1:a2].T, wi[:, a2:a3].T,
             jnp.zeros((SUBLANES, d), F32)], axis=0).astype(BF16)
        k, fk, qt, vt, ft, pool = _premix(
            x, mod[l], norm1_g[l].reshape(1, d), wku, wt, b_f[l].reshape(N_HEADS, 1),
            pool_w[l].astype(BF16), pool_scale[l].reshape(1, POOL_W), tri)
        attn = _attention(k, fk, qt, vt, ft)
        x = _ffn(x, attn, pool, mod[l], norm2_g[l].reshape(1, d), w_out[l].astype(BF16),
                 _chunk_gate_val(ffn_up[l]).astype(BF16), _chunk_gate_val(ffn_conv_w[l]),
                 _chunk_gate_val(ffn_conv_b[l]).reshape(1, 2 * D_FF), ffn_down[l].astype(BF16),
                 fg, final=(l == depth - 1))
    return x
```

```python
import functools

import jax
import jax.numpy as jnp
from jax import lax
from jax.experimental import pallas as pl
from jax.experimental.pallas import tpu as pltpu

F32 = jnp.float32
BF16 = jnp.bfloat16

D_MODEL = 1024
N_HEADS = 8
HEAD_DIM = 64
ATTN_W = N_HEADS * HEAD_DIM
POOL_WINDOWS = (2, 4, 8, 16)
POOL_C = 128
POOL_W = POOL_C * len(POOL_WINDOWS)
D_FF = 2816
CONV_W = 3
EPS = 1e-6

LANES = 128
SUBLANES = 8
HALO = 16
FF_CHUNK = 256
N_FF_CHUNKS = D_FF // FF_CHUNK
F_PIECES = 3
WT_F_ROWS = 16
ONES_LANE = 32
NEG_BIG = -1e30
LOG2E = 1.4426950408889634
V_ONES_ROWS = 16

TM_PREMIX = 512
TQ_ATTN = 512
TK_ATTN = 256
TM_FFN = 256
TN_MOD = 1024
VMEM_LIMIT = 56 * 1024 * 1024


def _split3(x):
    hi = x.astype(BF16).astype(F32)
    r = x - hi
    mid = r.astype(BF16).astype(F32)
    lo = (r - mid).astype(BF16).astype(F32)
    return hi, mid, lo


def _mod_kernel(c_ref, w_ref, b_ref, o_ref):
    c = c_ref[...]
    ca = (c * jax.nn.sigmoid(c)).astype(BF16)
    o_ref[...] = jnp.dot(ca, w_ref[...].astype(BF16), preferred_element_type=F32) + b_ref[...]


def _modulation(c, mod_w, mod_b):
    depth, d, n = mod_w.shape
    b = c.shape[0]
    return pl.pallas_call(
        _mod_kernel,
        grid=(depth, n // TN_MOD),
        in_specs=[
            pl.BlockSpec((b, d), lambda l, j: (0, 0)),
            pl.BlockSpec((None, d, TN_MOD), lambda l, j: (l, 0, j)),
            pl.BlockSpec((None, 1, TN_MOD), lambda l, j: (l, 0, j)),
        ],
        out_specs=pl.BlockSpec((None, b, TN_MOD), lambda l, j: (l, 0, j)),
        out_shape=jax.ShapeDtypeStruct((depth, b, n), F32),
        compiler_params=pltpu.CompilerParams(
            dimension_semantics=("arbitrary", "arbitrary"), vmem_limit_bytes=VMEM_LIMIT),
        name="modulation",
    )(c, mod_w, mod_b.reshape(depth, 1, n))


def _premix_kernel(x_ref, mod_ref, g_ref, wq_ref, wk_ref, wv_ref, wf_ref, wu_ref, bf_ref, pw_ref,
                   ps_ref, tri_ref, k_ref, fk_ref, qt_ref, vt_ref, ft_ref, pool_ref,
                   carry_ref, ubuf_ref, wku_ref, wt_ref, *, tm):
    i = pl.program_id(1)

    @pl.when((pl.program_id(0) == 0) & (i == 0))
    def _():
        wku_ref[:, :ATTN_W] = wk_ref[...].astype(BF16)
        wku_ref[:, ATTN_W:] = wu_ref[...].astype(BF16)
        wt_ref[0:ATTN_W, :] = (wq_ref[...] * (HEAD_DIM ** -0.5 * LOG2E)).T.astype(BF16)
        wt_ref[ATTN_W:2 * ATTN_W, :] = wv_ref[...].T.astype(BF16)
        wt_ref[2 * ATTN_W:, :] = wf_ref[...].T[0:WT_F_ROWS].astype(BF16)

    @pl.when(i == 0)
    def _():
        carry_ref[...] = jnp.zeros_like(carry_ref)
        ubuf_ref[0:HALO, :] = jnp.zeros((HALO, POOL_W), F32)

    x = x_ref[...]
    ms = jnp.mean(x * x, axis=-1, keepdims=True)
    scale = g_ref[...] * (1.0 + mod_ref[1:2, :])
    h = (x * lax.rsqrt(ms + EPS)) * scale + mod_ref[0:1, :]
    hb = h.astype(BF16)

    ku = jnp.dot(hb, wku_ref[...], preferred_element_type=F32)
    k_ref[...] = ku[:, :ATTN_W].astype(BF16)
    u = ku[:, ATTN_W:]

    zt = lax.dot_general(wt_ref[...], hb, (((1,), (1,)), ((), ())),
                         preferred_element_type=F32)
    qt_ref[...] = zt[0:ATTN_W].astype(BF16)
    vt_ref[...] = zt[ATTN_W:2 * ATTN_W].astype(BF16)

    fl = zt[2 * ATTN_W:2 * ATTN_W + N_HEADS] + bf_ref[...]
    logf = jnp.minimum(fl, 0.0) - jnp.log1p(jnp.exp(-jnp.abs(fl)))
    pieces = jnp.concatenate(list(_split3(logf)) + [jnp.zeros((SUBLANES, tm), F32)], axis=0)
    cs = jnp.dot(pieces.astype(BF16), tri_ref[...], preferred_element_type=F32)
    f_cum = (cs[0:8] + cs[8:16] + cs[16:24]) + carry_ref[:, 0:1]
    carry_ref[...] = jnp.broadcast_to(f_cum[:, tm - 1:tm], carry_ref.shape)
    f_log2 = f_cum * LOG2E
    ft_ref[...] = f_log2

    blk = jnp.concatenate(list(_split3(-f_log2))
                          + [jnp.zeros((LANES - F_PIECES * N_HEADS, tm), F32)], axis=0)
    blk_t = blk.T
    lane = lax.broadcasted_iota(jnp.int32, (tm, LANES), 1)
    blk_t = jnp.where((lane >= ONES_LANE) & (lane < ONES_LANE + F_PIECES), 1.0, blk_t)
    fk_ref[...] = blk_t.astype(BF16)

    ubuf_ref[HALO:HALO + tm, :] = u
    pos = i * tm + lax.broadcasted_iota(jnp.int32, (tm, POOL_C), 0)
    for g, w in enumerate(POOL_WINDOWS):
        c0 = g * POOL_C
        ug = u[:, c0:c0 + POOL_C]
        ws = ug
        for j in range(1, w):
            ws = ws + ubuf_ref[HALO - j:HALO - j + tm, c0:c0 + POOL_C]
        cnt = jnp.minimum(pos + 1, w).astype(F32)
        d = ws / cnt - ug
        pg = jnp.dot(d.astype(BF16), pw_ref[g], preferred_element_type=F32)
        pool_ref[:, c0:c0 + POOL_C] = (pg * ps_ref[:, c0:c0 + POOL_C]).astype(BF16)
    ubuf_ref[0:HALO, :] = ubuf_ref[tm:tm + HALO, :]


def _premix(x, mod, g1, w_in, layer, wf, wu, b_f, pool_w, pool_scale, tri):
    b, s, d = x.shape
    tm = TM_PREMIX
    once = pl.Buffered(1)
    w_cols = lambda j: pl.BlockSpec((None, d, ATTN_W), lambda bi, i: (layer, 0, j), pipeline_mode=once)
    kern = functools.partial(_premix_kernel, tm=tm)
    row = lambda bi, i: (bi, i, 0)
    col = lambda bi, i: (bi, 0, i)
    const2 = lambda bi, i: (0, 0)
    return pl.pallas_call(
        kern,
        grid=(b, s // tm),
        in_specs=[
            pl.BlockSpec((None, tm, d), row),
            pl.BlockSpec((None, 6, d), lambda bi, i: (bi, 0, 0)),
            pl.BlockSpec((1, d), const2),
            w_cols(0), w_cols(1), w_cols(2),
            pl.BlockSpec((d, LANES), const2, pipeline_mode=once),
            pl.BlockSpec((d, POOL_W), const2, pipeline_mode=once),
            pl.BlockSpec((N_HEADS, 1), const2),
            pl.BlockSpec((len(POOL_WINDOWS), POOL_C, POOL_C), lambda bi, i: (0, 0, 0)),
            pl.BlockSpec((1, POOL_W), const2),
            pl.BlockSpec((tm, tm), const2),
        ],
        out_specs=[
            pl.BlockSpec((None, tm, ATTN_W), row),
            pl.BlockSpec((None, tm, LANES), row),
            pl.BlockSpec((None, ATTN_W, tm), col),
            pl.BlockSpec((None, ATTN_W, tm), col),
            pl.BlockSpec((None, N_HEADS, tm), col),
            pl.BlockSpec((None, tm, POOL_W), row),
        ],
        out_shape=[
            jax.ShapeDtypeStruct((b, s, ATTN_W), BF16),
            jax.ShapeDtypeStruct((b, s, LANES), BF16),
            jax.ShapeDtypeStruct((b, ATTN_W, s), BF16),
            jax.ShapeDtypeStruct((b, ATTN_W, s), BF16),
            jax.ShapeDtypeStruct((b, N_HEADS, s), F32),
            jax.ShapeDtypeStruct((b, s, POOL_W), BF16),
        ],
        scratch_shapes=[
            pltpu.VMEM((N_HEADS, LANES), F32),
            pltpu.VMEM((HALO + tm, POOL_W), F32),
            pltpu.VMEM((d, ATTN_W + POOL_W), BF16),
            pltpu.VMEM((2 * ATTN_W + WT_F_ROWS, d), BF16),
        ],
        compiler_params=pltpu.CompilerParams(
            dimension_semantics=("arbitrary", "arbitrary"), vmem_limit_bytes=VMEM_LIMIT),
        name="premix",
    )(x, mod, g1, w_in, w_in, w_in, wf, wu, b_f, pool_w, pool_scale, tri)


def _attn_kernel(k_ref, fk_ref, qt_ref, vt_ref, ft_ref, o_ref, s_scr, p_scr, *, tq, tk, seq):
    assert tq == 2 * tk
    p = pl.program_id(1)
    row = lax.broadcasted_iota(jnp.int32, (LANES, tq), 0)
    key_pos = lax.broadcasted_iota(jnp.int32, (tk, tq), 0)
    qry_pos = lax.broadcasted_iota(jnp.int32, (tk, tq), 1)
    ones_rows = jnp.ones((V_ONES_ROWS, tk), BF16)

    def scores(t, q_all):
        r0 = t * tk
        k_aug = jnp.concatenate([k_ref[r0:r0 + tk, :], fk_ref[r0:r0 + tk, :]], axis=1)
        s_scr[t % 2] = jnp.dot(k_aug, q_all, preferred_element_type=F32)

    def softmax(t, ms, mask):
        new_ms, alphas = [], []
        for hh in range(2):
            s = s_scr[t % 2, :, hh * tq:(hh + 1) * tq]
            if mask is not None:
                s = jnp.where(mask, s, -jnp.inf)
            m_new = jnp.maximum(ms[hh], jnp.max(s, axis=0, keepdims=True))
            alphas.append(jnp.exp2(ms[hh] - m_new))
            new_ms.append(m_new)
            p_scr[t % 2, :, hh * tq:(hh + 1) * tq] = jnp.exp2(s - m_new).astype(BF16)
        return new_ms, alphas

    def accumulate(t, alphas, accs):
        r0 = t * tk
        out = []
        for hh in range(2):
            v_aug = jnp.concatenate(
                [vt_ref[hh * HEAD_DIM:(hh + 1) * HEAD_DIM, r0:r0 + tk], ones_rows], axis=0)
            pv = jnp.dot(v_aug, p_scr[t % 2, :, hh * tq:(hh + 1) * tq],
                         preferred_element_type=F32)
            out.append(alphas[hh] * accs[hh] + pv)
        return out

    for qb in range(seq // tq):
        c0 = qb * tq
        qt2 = qt_ref[:, c0:c0 + tq]
        q_augs = []
        for hh in range(2):
            head = 2 * p + hh
            q_h = jnp.where((row >= hh * HEAD_DIM) & (row < (hh + 1) * HEAD_DIM), qt2,
                            jnp.zeros_like(qt2))
            pieces = _split3(ft_ref[hh:hh + 1, c0:c0 + tq])
            fblk = jnp.where((row < F_PIECES * N_HEADS) & (row % N_HEADS == head), 1.0, 0.0)
            for j in range(F_PIECES):
                fblk = jnp.where(row == ONES_LANE + j, pieces[j], fblk)
            q_augs.append(jnp.concatenate([q_h, fblk.astype(BF16)], axis=0))
        q_all = jnp.concatenate(q_augs, axis=1)

        n_full = 2 * qb
        n_tiles = n_full + 2
        ms = [jnp.full((1, tq), NEG_BIG, F32) for _ in range(2)]
        accs = [jnp.zeros((HEAD_DIM + V_ONES_ROWS, tq), F32) for _ in range(2)]
        pending = None
        scores(0, q_all)
        for t in range(n_tiles):
            if t + 1 < n_tiles:
                scores(t + 1, q_all)
            mask = None if t < n_full else (key_pos + (t - n_full) * tk <= qry_pos)
            ms, alphas = softmax(t, ms, mask)
            if pending is not None:
                accs = accumulate(*pending, accs)
            pending = (t, alphas)
        accs = accumulate(*pending, accs)
        out_t = jnp.concatenate(
            [a[:HEAD_DIM] / a[HEAD_DIM:HEAD_DIM + 1] for a in accs], axis=0)
        o_ref[c0:c0 + tq, :] = out_t.T.astype(BF16)


def _attention(k, fk, qt, vt, ft):
    b, s, _ = k.shape
    n_pairs = N_HEADS // 2
    kern = functools.partial(_attn_kernel, tq=TQ_ATTN, tk=TK_ATTN, seq=s)
    return pl.pallas_call(
        kern,
        grid=(b, n_pairs),
        in_specs=[
            pl.BlockSpec((None, s, LANES), lambda bi, p: (bi, 0, p)),
            pl.BlockSpec((None, s, LANES), lambda bi, p: (bi, 0, 0)),
            pl.BlockSpec((None, LANES, s), lambda bi, p: (bi, p, 0)),
            pl.BlockSpec((None, LANES, s), lambda bi, p: (bi, p, 0)),
            pl.BlockSpec((None, None, 2, s), lambda bi, p: (bi, p, 0, 0)),
        ],
        out_specs=pl.BlockSpec((None, s, LANES), lambda bi, p: (bi, 0, p)),
        out_shape=jax.ShapeDtypeStruct((b, s, ATTN_W), BF16),
        scratch_shapes=[
            pltpu.VMEM((2, TK_ATTN, 2 * TQ_ATTN), F32),
            pltpu.VMEM((2, TK_ATTN, 2 * TQ_ATTN), BF16),
        ],
        compiler_params=pltpu.CompilerParams(
            dimension_semantics=("arbitrary", "arbitrary"), vmem_limit_bytes=VMEM_LIMIT),
        name="fox_attention",
    )(k, fk, qt, vt, ft.reshape(b, n_pairs, 2, s))


def _ffn_kernel(x_ref, attn_ref, pool_ref, mod_ref, g_ref, wo_ref, wup_ref, cw_ref, cb_ref,
                wdn_ref, fg_ref, o_ref, h_ref, act_ref, carry_ref, *, tm, final):
    i = pl.program_id(1)

    @pl.when(i == 0)
    def _():
        carry_ref[...] = jnp.zeros_like(carry_ref)

    mixed = jnp.concatenate([attn_ref[...], pool_ref[...]], axis=1)
    y = jnp.dot(mixed, wo_ref[...], preferred_element_type=F32)
    x1 = x_ref[...] + mod_ref[2:3, :] * y
    o_ref[...] = x1
    ms = jnp.mean(x1 * x1, axis=-1, keepdims=True)
    scale = g_ref[...] * (1.0 + mod_ref[4:5, :])
    h_ref[...] = ((x1 * lax.rsqrt(ms + EPS)) * scale + mod_ref[3:4, :]).astype(BF16)

    def conv_branch(c0):
        cols = slice(c0, c0 + FF_CHUNK)
        a = jnp.dot(h_ref[...], wup_ref[:, cols], preferred_element_type=F32)
        ext = jnp.concatenate([carry_ref[:, cols], a], axis=0)
        carry_ref[:, cols] = a[tm - SUBLANES:tm, :]
        a1 = pltpu.roll(ext, 1, axis=0)[SUBLANES:]
        a2 = pltpu.roll(ext, 2, axis=0)[SUBLANES:]
        return (cb_ref[:, cols] + a2 * cw_ref[0:1, cols] + a1 * cw_ref[1:2, cols]
                + a * cw_ref[2:3, cols])

    for c in range(N_FF_CHUNKS):
        gate = conv_branch(c * FF_CHUNK)
        val = conv_branch(D_FF + c * FF_CHUNK)
        act_ref[:, c * FF_CHUNK:(c + 1) * FF_CHUNK] = (gate * jax.nn.sigmoid(gate) * val).astype(BF16)

    y2 = jnp.dot(act_ref[...], wdn_ref[...], preferred_element_type=F32)
    x2 = o_ref[...] + mod_ref[5:6, :] * y2
    if final:
        ms2 = jnp.mean(x2 * x2, axis=-1, keepdims=True)
        x2 = (x2 * lax.rsqrt(ms2 + EPS)) * fg_ref[...]
    o_ref[...] = x2


def _ffn(x, attn, pool, mod, g2, wo, wup, cw, cb, wdn, fg, final):
    b, s, d = x.shape
    tm = TM_FFN
    kern = functools.partial(_ffn_kernel, tm=tm, final=final)
    row = lambda bi, i: (bi, i, 0)
    const2 = lambda bi, i: (0, 0)
    once = pl.Buffered(1)
    return pl.pallas_call(
        kern,
        grid=(b, s // tm),
        in_specs=[
            pl.BlockSpec((None, tm, d), row),
            pl.BlockSpec((None, tm, ATTN_W), row),
            pl.BlockSpec((None, tm, POOL_W), row),
            pl.BlockSpec((None, 6, d), lambda bi, i: (bi, 0, 0)),
            pl.BlockSpec((1, d), const2),
            pl.BlockSpec((d, d), const2, pipeline_mode=once),
            pl.BlockSpec((d, 2 * D_FF), const2, pipeline_mode=once),
            pl.BlockSpec((CONV_W, 2 * D_FF), const2),
            pl.BlockSpec((1, 2 * D_FF), const2),
            pl.BlockSpec((D_FF, d), const2, pipeline_mode=once),
            pl.BlockSpec((1, d), const2),
        ],
        out_specs=pl.BlockSpec((None, tm, d), row),
        out_shape=jax.ShapeDtypeStruct((b, s, d), F32),
        scratch_shapes=[
            pltpu.VMEM((tm, d), BF16),
            pltpu.VMEM((tm, D_FF), BF16),
            pltpu.VMEM((SUBLANES, 2 * D_FF), F32),
        ],
        compiler_params=pltpu.CompilerParams(
            dimension_semantics=("arbitrary", "arbitrary"), vmem_limit_bytes=VMEM_LIMIT),
        name="outproj_convffn",
    )(x, attn, pool, mod, g2, wo, wup, cw, cb, wdn, fg)


def kernel(x, c, mod_w, mod_b, norm1_g, norm2_g, w_in, b_f, pool_w, pool_scale, w_out, ffn_up,
           ffn_conv_w, ffn_conv_b, ffn_down, final_g):
    b, s, d = x.shape
    depth = mod_w.shape[0]
    assert (d, s % TM_PREMIX, s % TQ_ATTN, s % TM_FFN) == (D_MODEL, 0, 0, 0)

    mod = _modulation(c, mod_w, mod_b).reshape(depth, b, 6, d)
    tri = (lax.broadcasted_iota(jnp.int32, (TM_PREMIX, TM_PREMIX), 0)
           <= lax.broadcasted_iota(jnp.int32, (TM_PREMIX, TM_PREMIX), 1)).astype(BF16)
    a2, a3 = 3 * ATTN_W, 3 * ATTN_W + N_HEADS
    fg = final_g.reshape(1, d)

    for l in range(depth):
        wf = jnp.pad(w_in[l, :, a2:a3], ((0, 0), (0, LANES - N_HEADS)))
        wu = w_in[l, :, a3:]
        k, fk, qt, vt, ft, pool = _premix(
            x, mod[l], norm1_g[l].reshape(1, d), w_in, l, wf, wu, b_f[l].reshape(N_HEADS, 1),
            pool_w[l].astype(BF16), pool_scale[l].reshape(1, POOL_W), tri)
        attn = _attention(k, fk, qt, vt, ft)
        x = _ffn(x, attn, pool, mod[l], norm2_g[l].reshape(1, d), w_out[l].astype(BF16),
                 ffn_up[l].astype(BF16), ffn_conv_w[l], ffn_conv_b[l].reshape(1, 2 * D_FF),
                 ffn_down[l].astype(BF16), fg, final=(l == depth - 1))
    return x
```

```python
import functools

import jax
import jax.numpy as jnp
from jax import lax
from jax.experimental import pallas as pl
from jax.experimental.pallas import tpu as pltpu

F32 = jnp.float32
BF16 = jnp.bfloat16

D_MODEL = 1024
N_HEADS = 8
HEAD_DIM = 64
ATTN_W = N_HEADS * HEAD_DIM
POOL_WINDOWS = (2, 4, 8, 16)
POOL_C = 128
POOL_W = POOL_C * len(POOL_WINDOWS)
D_FF = 2816
CONV_W = 3
EPS = 1e-6

LANES = 128
SUBLANES = 8
HALO = 16
FF_CHUNK = 256
N_FF_CHUNKS = D_FF // FF_CHUNK
F_PIECES = 3
WT_F_ROWS = 16
ONES_LANE = 32
NEG_BIG = -1e30
LOG2E = 1.4426950408889634
V_ONES_ROWS = 16

TM_PREMIX = 512
TQ_ATTN = 512
TK_ATTN = 256
TM_FFN = 512
TN_MOD = 1024
VMEM_LIMIT = 56 * 1024 * 1024


def _split3(x):
    hi = x.astype(BF16).astype(F32)
    r = x - hi
    mid = r.astype(BF16).astype(F32)
    lo = (r - mid).astype(BF16).astype(F32)
    return hi, mid, lo


def _mod_kernel(c_ref, w_ref, b_ref, o_ref):
    c = c_ref[...]
    ca = (c * jax.nn.sigmoid(c)).astype(BF16)
    o_ref[...] = jnp.dot(ca, w_ref[...].astype(BF16), preferred_element_type=F32) + b_ref[...]


def _modulation(c, mod_w, mod_b):
    depth, d, n = mod_w.shape
    b = c.shape[0]
    return pl.pallas_call(
        _mod_kernel,
        grid=(depth, n // TN_MOD),
        in_specs=[
            pl.BlockSpec((b, d), lambda l, j: (0, 0)),
            pl.BlockSpec((None, d, TN_MOD), lambda l, j: (l, 0, j)),
            pl.BlockSpec((None, 1, TN_MOD), lambda l, j: (l, 0, j)),
        ],
        out_specs=pl.BlockSpec((None, b, TN_MOD), lambda l, j: (l, 0, j)),
        out_shape=jax.ShapeDtypeStruct((depth, b, n), F32),
        compiler_params=pltpu.CompilerParams(
            dimension_semantics=("arbitrary", "arbitrary"), vmem_limit_bytes=VMEM_LIMIT),
        name="modulation",
    )(c, mod_w, mod_b.reshape(depth, 1, n))


def _premix_kernel(x_ref, mod_ref, g_ref, w_ref, bf_ref, pw_ref, ps_ref, tri_ref,
                   k_ref, fk_ref, qt_ref, vt_ref, ft_ref, pool_ref,
                   carry_ref, ubuf_ref, wku_ref, wt_ref, *, tm):
    i = pl.program_id(1)

    @pl.when((pl.program_id(0) == 0) & (i == 0))
    def _():
        a0, a1, a2, a3 = ATTN_W, 2 * ATTN_W, 3 * ATTN_W, 3 * ATTN_W + N_HEADS
        wku_ref[:ATTN_W, :] = w_ref[a0:a1, :].astype(BF16)
        wku_ref[ATTN_W:, :] = w_ref[a3:, :].astype(BF16)
        wt_ref[0:ATTN_W, :] = (w_ref[0:a0, :] * (HEAD_DIM ** -0.5 * LOG2E)).astype(BF16)
        wt_ref[ATTN_W:2 * ATTN_W, :] = w_ref[a1:a2, :].astype(BF16)
        wt_ref[2 * ATTN_W:, :] = jnp.concatenate(
            [w_ref[a2:a3, :], jnp.zeros((WT_F_ROWS - N_HEADS, w_ref.shape[1]), F32)], axis=0).astype(BF16)

    @pl.when(i == 0)
    def _():
        carry_ref[...] = jnp.zeros_like(carry_ref)
        ubuf_ref[0:HALO, :] = jnp.zeros((HALO, POOL_W), F32)

    x = x_ref[...]
    ms = jnp.mean(x * x, axis=-1, keepdims=True)
    scale = g_ref[...] * (1.0 + mod_ref[1:2, :])
    h = (x * lax.rsqrt(ms + EPS)) * scale + mod_ref[0:1, :]
    hb = h.astype(BF16)

    ku = lax.dot_general(hb, wku_ref[...], (((1,), (1,)), ((), ())),
                         preferred_element_type=F32)
    k_ref[...] = ku[:, :ATTN_W].astype(BF16)
    u = ku[:, ATTN_W:]

    zt = lax.dot_general(wt_ref[...], hb, (((1,), (1,)), ((), ())),
                         preferred_element_type=F32)
    qt_ref[...] = zt[0:ATTN_W].astype(BF16)
    vt_ref[...] = zt[ATTN_W:2 * ATTN_W].astype(BF16)

    fl = zt[2 * ATTN_W:2 * ATTN_W + N_HEADS] + bf_ref[...]
    logf = jnp.minimum(fl, 0.0) - jnp.log1p(jnp.exp(-jnp.abs(fl)))
    pieces = jnp.concatenate(list(_split3(logf)) + [jnp.zeros((SUBLANES, tm), F32)], axis=0)
    cs = jnp.dot(pieces.astype(BF16), tri_ref[...], preferred_element_type=F32)
    f_cum = (cs[0:8] + cs[8:16] + cs[16:24]) + carry_ref[:, 0:1]
    carry_ref[...] = jnp.broadcast_to(f_cum[:, tm - 1:tm], carry_ref.shape)
    f_log2 = f_cum * LOG2E
    ft_ref[...] = f_log2

    blk = jnp.concatenate(list(_split3(-f_log2))
                          + [jnp.zeros((LANES - F_PIECES * N_HEADS, tm), F32)], axis=0)
    blk_t = blk.T
    lane = lax.broadcasted_iota(jnp.int32, (tm, LANES), 1)
    blk_t = jnp.where((lane >= ONES_LANE) & (lane < ONES_LANE + F_PIECES), 1.0, blk_t)
    fk_ref[...] = blk_t.astype(BF16)

    ubuf_ref[HALO:HALO + tm, :] = u
    pos = i * tm + lax.broadcasted_iota(jnp.int32, (tm, POOL_C), 0)
    for g, w in enumerate(POOL_WINDOWS):
        c0 = g * POOL_C
        ug = u[:, c0:c0 + POOL_C]
        ws = ug
        for j in range(1, w):
            ws = ws + ubuf_ref[HALO - j:HALO - j + tm, c0:c0 + POOL_C]
        cnt = jnp.minimum(pos + 1, w).astype(F32)
        d = ws / cnt - ug
        pg = jnp.dot(d.astype(BF16), pw_ref[g], preferred_element_type=F32)
        pool_ref[:, c0:c0 + POOL_C] = (pg * ps_ref[:, c0:c0 + POOL_C]).astype(BF16)
    ubuf_ref[0:HALO, :] = ubuf_ref[tm:tm + HALO, :]


def _premix(x, mod, g1, w_in_t, layer, b_f, pool_w, pool_scale, tri):
    b, s, d = x.shape
    tm = TM_PREMIX
    once = pl.Buffered(1)
    kern = functools.partial(_premix_kernel, tm=tm)
    row = lambda bi, i: (bi, i, 0)
    col = lambda bi, i: (bi, 0, i)
    const2 = lambda bi, i: (0, 0)
    return pl.pallas_call(
        kern,
        grid=(b, s // tm),
        in_specs=[
            pl.BlockSpec((None, tm, d), row),
            pl.BlockSpec((None, 6, d), lambda bi, i: (bi, 0, 0)),
            pl.BlockSpec((1, d), const2),
            pl.BlockSpec((None, w_in_t.shape[1], d), lambda bi, i: (layer, 0, 0), pipeline_mode=once),
            pl.BlockSpec((N_HEADS, 1), const2),
            pl.BlockSpec((len(POOL_WINDOWS), POOL_C, POOL_C), lambda bi, i: (0, 0, 0)),
            pl.BlockSpec((1, POOL_W), const2),
            pl.BlockSpec((tm, tm), const2),
        ],
        out_specs=[
            pl.BlockSpec((None, tm, ATTN_W), row),
            pl.BlockSpec((None, tm, LANES), row),
            pl.BlockSpec((None, ATTN_W, tm), col),
            pl.BlockSpec((None, ATTN_W, tm), col),
            pl.BlockSpec((None, N_HEADS, tm), col),
            pl.BlockSpec((None, tm, POOL_W), row),
        ],
        out_shape=[
            jax.ShapeDtypeStruct((b, s, ATTN_W), BF16),
            jax.ShapeDtypeStruct((b, s, LANES), BF16),
            jax.ShapeDtypeStruct((b, ATTN_W, s), BF16),
            jax.ShapeDtypeStruct((b, ATTN_W, s), BF16),
            jax.ShapeDtypeStruct((b, N_HEADS, s), F32),
            jax.ShapeDtypeStruct((b, s, POOL_W), BF16),
        ],
        scratch_shapes=[
            pltpu.VMEM((N_HEADS, LANES), F32),
            pltpu.VMEM((HALO + tm, POOL_W), F32),
            pltpu.VMEM((ATTN_W + POOL_W, d), BF16),
            pltpu.VMEM((2 * ATTN_W + WT_F_ROWS, d), BF16),
        ],
        compiler_params=pltpu.CompilerParams(
            dimension_semantics=("arbitrary", "arbitrary"), vmem_limit_bytes=VMEM_LIMIT),
        name="premix",
    )(x, mod, g1, w_in_t, b_f, pool_w, pool_scale, tri)


def _attn_kernel(k_ref, fk_ref, qt_ref, vt_ref, ft_ref, o_ref, s_scr, p_scr, *, tq, tk, seq):
    assert tq == 2 * tk
    p = pl.program_id(1)
    row = lax.broadcasted_iota(jnp.int32, (LANES, tq), 0)
    key_pos = lax.broadcasted_iota(jnp.int32, (tk, tq), 0)
    qry_pos = lax.broadcasted_iota(jnp.int32, (tk, tq), 1)
    ones_rows = jnp.ones((V_ONES_ROWS, tk), BF16)

    def scores(t, q_all):
        r0 = t * tk
        k_aug = jnp.concatenate([k_ref[r0:r0 + tk, :], fk_ref[r0:r0 + tk, :]], axis=1)
        s_scr[t % 2] = jnp.dot(k_aug, q_all, preferred_element_type=F32)

    def softmax(t, ms, mask):
        new_ms, alphas = [], []
        for hh in range(2):
            s = s_scr[t % 2, :, hh * tq:(hh + 1) * tq]
            if mask is not None:
                s = jnp.where(mask, s, -jnp.inf)
            m_new = jnp.maximum(ms[hh], jnp.max(s, axis=0, keepdims=True))
            alphas.append(jnp.exp2(ms[hh] - m_new))
            new_ms.append(m_new)
            p_scr[t % 2, :, hh * tq:(hh + 1) * tq] = jnp.exp2(s - m_new).astype(BF16)
        return new_ms, alphas

    def accumulate(t, alphas, accs):
        r0 = t * tk
        out = []
        for hh in range(2):
            v_aug = jnp.concatenate(
                [vt_ref[hh * HEAD_DIM:(hh + 1) * HEAD_DIM, r0:r0 + tk], ones_rows], axis=0)
            pv = jnp.dot(v_aug, p_scr[t % 2, :, hh * tq:(hh + 1) * tq],
                         preferred_element_type=F32)
            out.append(alphas[hh] * accs[hh] + pv)
        return out

    for qb in range(seq // tq):
        c0 = qb * tq
        qt2 = qt_ref[:, c0:c0 + tq]
        q_augs = []
        for hh in range(2):
            head = 2 * p + hh
            q_h = jnp.where((row >= hh * HEAD_DIM) & (row < (hh + 1) * HEAD_DIM), qt2,
                            jnp.zeros_like(qt2))
            pieces = _split3(ft_ref[hh:hh + 1, c0:c0 + tq])
            fblk = jnp.where((row < F_PIECES * N_HEADS) & (row % N_HEADS == head), 1.0, 0.0)
            for j in range(F_PIECES):
                fblk = jnp.where(row == ONES_LANE + j, pieces[j], fblk)
            q_augs.append(jnp.concatenate([q_h, fblk.astype(BF16)], axis=0))
        q_all = jnp.concatenate(q_augs, axis=1)

        n_full = 2 * qb
        n_tiles = n_full + 2
        ms = [jnp.full((1, tq), NEG_BIG, F32) for _ in range(2)]
        accs = [jnp.zeros((HEAD_DIM + V_ONES_ROWS, tq), F32) for _ in range(2)]
        pending = None
        scores(0, q_all)
        for t in range(n_tiles):
            if t + 1 < n_tiles:
                scores(t + 1, q_all)
            mask = None if t < n_full else (key_pos + (t - n_full) * tk <= qry_pos)
            ms, alphas = softmax(t, ms, mask)
            if pending is not None:
                accs = accumulate(*pending, accs)
            pending = (t, alphas)
        accs = accumulate(*pending, accs)
        out_t = jnp.concatenate(
            [a[:HEAD_DIM] / a[HEAD_DIM:HEAD_DIM + 1] for a in accs], axis=0)
        o_ref[c0:c0 + tq, :] = out_t.T.astype(BF16)


def _attention(k, fk, qt, vt, ft):
    b, s, _ = k.shape
    n_pairs = N_HEADS // 2
    kern = functools.partial(_attn_kernel, tq=TQ_ATTN, tk=TK_ATTN, seq=s)
    return pl.pallas_call(
        kern,
        grid=(b, n_pairs),
        in_specs=[
            pl.BlockSpec((None, s, LANES), lambda bi, p: (bi, 0, p)),
            pl.BlockSpec((None, s, LANES), lambda bi, p: (bi, 0, 0)),
            pl.BlockSpec((None, LANES, s), lambda bi, p: (bi, p, 0)),
            pl.BlockSpec((None, LANES, s), lambda bi, p: (bi, p, 0)),
            pl.BlockSpec((None, None, 2, s), lambda bi, p: (bi, p, 0, 0)),
        ],
        out_specs=pl.BlockSpec((None, s, LANES), lambda bi, p: (bi, 0, p)),
        out_shape=jax.ShapeDtypeStruct((b, s, ATTN_W), BF16),
        scratch_shapes=[
            pltpu.VMEM((2, TK_ATTN, 2 * TQ_ATTN), F32),
            pltpu.VMEM((2, TK_ATTN, 2 * TQ_ATTN), BF16),
        ],
        compiler_params=pltpu.CompilerParams(
            dimension_semantics=("arbitrary", "arbitrary"), vmem_limit_bytes=VMEM_LIMIT),
        name="fox_attention",
    )(k, fk, qt, vt, ft.reshape(b, n_pairs, 2, s))


def _ffn_kernel(x_ref, attn_ref, pool_ref, mod_ref, g_ref, wo_ref, wup_ref, cw_ref, cb_ref,
                wdn_ref, fg_ref, o_ref, h_ref, act_ref, carry_ref, *, tm, final):
    i = pl.program_id(1)

    @pl.when(i == 0)
    def _():
        carry_ref[...] = jnp.zeros_like(carry_ref)

    mixed = jnp.concatenate([attn_ref[...], pool_ref[...]], axis=1)
    y = jnp.dot(mixed, wo_ref[...], preferred_element_type=F32)
    x1 = x_ref[...] + mod_ref[2:3, :] * y
    o_ref[...] = x1
    ms = jnp.mean(x1 * x1, axis=-1, keepdims=True)
    scale = g_ref[...] * (1.0 + mod_ref[4:5, :])
    h_ref[...] = ((x1 * lax.rsqrt(ms + EPS)) * scale + mod_ref[3:4, :]).astype(BF16)

    def conv_branch(c0):
        cols = slice(c0, c0 + FF_CHUNK)
        a = jnp.dot(h_ref[...], wup_ref[:, cols], preferred_element_type=F32)
        ext = jnp.concatenate([carry_ref[:, cols], a], axis=0)
        carry_ref[:, cols] = a[tm - SUBLANES:tm, :]
        a1 = pltpu.roll(ext, 1, axis=0)[SUBLANES:]
        a2 = pltpu.roll(ext, 2, axis=0)[SUBLANES:]
        return (cb_ref[:, cols] + a2 * cw_ref[0:1, cols] + a1 * cw_ref[1:2, cols]
                + a * cw_ref[2:3, cols])

    for c in range(N_FF_CHUNKS):
        gate = conv_branch(c * FF_CHUNK)
        val = conv_branch(D_FF + c * FF_CHUNK)
        act_ref[:, c * FF_CHUNK:(c + 1) * FF_CHUNK] = (gate * jax.nn.sigmoid(gate) * val).astype(BF16)

    y2 = jnp.dot(act_ref[...], wdn_ref[...], preferred_element_type=F32)
    x2 = o_ref[...] + mod_ref[5:6, :] * y2
    if final:
        ms2 = jnp.mean(x2 * x2, axis=-1, keepdims=True)
        x2 = (x2 * lax.rsqrt(ms2 + EPS)) * fg_ref[...]
    o_ref[...] = x2


def _ffn(x, attn, pool, mod, g2, wo, wup, cw, cb, wdn, fg, layer, final):
    b, s, d = x.shape
    tm = TM_FFN
    kern = functools.partial(_ffn_kernel, tm=tm, final=final)
    row = lambda bi, i: (bi, i, 0)
    const2 = lambda bi, i: (0, 0)
    per_layer = lambda bi, i: (layer, 0, 0)
    once = pl.Buffered(1)
    return pl.pallas_call(
        kern,
        grid=(b, s // tm),
        in_specs=[
            pl.BlockSpec((None, tm, d), row),
            pl.BlockSpec((None, tm, ATTN_W), row),
            pl.BlockSpec((None, tm, POOL_W), row),
            pl.BlockSpec((None, 6, d), lambda bi, i: (bi, 0, 0)),
            pl.BlockSpec((1, d), const2),
            pl.BlockSpec((None, d, d), per_layer, pipeline_mode=once),
            pl.BlockSpec((None, d, 2 * D_FF), per_layer, pipeline_mode=once),
            pl.BlockSpec((None, CONV_W, 2 * D_FF), per_layer),
            pl.BlockSpec((None, 1, 2 * D_FF), per_layer),
            pl.BlockSpec((None, D_FF, d), per_layer, pipeline_mode=once),
            pl.BlockSpec((1, d), const2),
        ],
        out_specs=pl.BlockSpec((None, tm, d), row),
        out_shape=jax.ShapeDtypeStruct((b, s, d), F32),
        scratch_shapes=[
            pltpu.VMEM((tm, d), BF16),
            pltpu.VMEM((tm, D_FF), BF16),
            pltpu.VMEM((SUBLANES, 2 * D_FF), F32),
        ],
        compiler_params=pltpu.CompilerParams(
            dimension_semantics=("arbitrary", "arbitrary"), vmem_limit_bytes=VMEM_LIMIT),
        name="outproj_convffn",
    )(x, attn, pool, mod, g2, wo, wup, cw, cb, wdn, fg)


def kernel(x, c, mod_w, mod_b, norm1_g, norm2_g, w_in, b_f, pool_w, pool_scale, w_out, ffn_up,
           ffn_conv_w, ffn_conv_b, ffn_down, final_g):
    b, s, d = x.shape
    depth = mod_w.shape[0]
    assert (d, s % TM_PREMIX, s % TQ_ATTN, s % TM_FFN) == (D_MODEL, 0, 0, 0)

    mod = _modulation(c, mod_w, mod_b).reshape(depth, b, 6, d)
    tri = (lax.broadcasted_iota(jnp.int32, (TM_PREMIX, TM_PREMIX), 0)
           <= lax.broadcasted_iota(jnp.int32, (TM_PREMIX, TM_PREMIX), 1)).astype(BF16)
    fg = final_g.reshape(1, d)
    wo, wup, wdn = w_out.astype(BF16), ffn_up.astype(BF16), ffn_down.astype(BF16)
    cb = ffn_conv_b.reshape(depth, 1, 2 * D_FF)
    w_in_t = jnp.swapaxes(w_in, 1, 2)

    for l in range(depth):
        k, fk, qt, vt, ft, pool = _premix(
            x, mod[l], norm1_g[l].reshape(1, d), w_in_t, l, b_f[l].reshape(N_HEADS, 1),
            pool_w[l].astype(BF16), pool_scale[l].reshape(1, POOL_W), tri)
        attn = _attention(k, fk, qt, vt, ft)
        x = _ffn(x, attn, pool, mod[l], norm2_g[l].reshape(1, d), wo, wup, ffn_conv_w, cb, wdn, fg,
                 layer=l, final=(l == depth - 1))
    return x
```

```python
import functools

import jax
import jax.numpy as jnp
from jax import lax
from jax.experimental import pallas as pl
from jax.experimental.pallas import tpu as pltpu

F32 = jnp.float32
BF16 = jnp.bfloat16

D_MODEL = 1024
N_HEADS = 8
HEAD_DIM = 64
ATTN_W = N_HEADS * HEAD_DIM
POOL_WINDOWS = (2, 4, 8, 16)
POOL_C = 128
POOL_W = POOL_C * len(POOL_WINDOWS)
D_FF = 2816
CONV_W = 3
EPS = 1e-6

LANES = 128
SUBLANES = 8
BF16_SUBLANES = 16
HALO = 16
FF_CHUNK = 256
N_FF_CHUNKS = D_FF // FF_CHUNK
F_PIECES = 3
WT_F_ROWS = 16
ONES_LANE = 32
NEG_BIG = -1e30
LOG2E = 1.4426950408889634
V_ONES_ROWS = 16

TM_PREMIX = 512
TQ_ATTN = 512
TK_ATTN = 256
TM_FFN = 512
TN_MOD = 1024
VMEM_LIMIT = 56 * 1024 * 1024


def _split3(x):
    hi = x.astype(BF16).astype(F32)
    r = x - hi
    mid = r.astype(BF16).astype(F32)
    lo = (r - mid).astype(BF16).astype(F32)
    return hi, mid, lo


def _mod_kernel(c_ref, w_ref, b_ref, o_ref):
    c = c_ref[...]
    ca = (c * jax.nn.sigmoid(c)).astype(BF16)
    o_ref[...] = jnp.dot(ca, w_ref[...].astype(BF16), preferred_element_type=F32) + b_ref[...]


def _modulation(c, mod_w, mod_b):
    depth, d, n = mod_w.shape
    b = c.shape[0]
    return pl.pallas_call(
        _mod_kernel,
        grid=(depth, n // TN_MOD),
        in_specs=[
            pl.BlockSpec((b, d), lambda l, j: (0, 0)),
            pl.BlockSpec((None, d, TN_MOD), lambda l, j: (l, 0, j)),
            pl.BlockSpec((None, 1, TN_MOD), lambda l, j: (l, 0, j)),
        ],
        out_specs=pl.BlockSpec((None, b, TN_MOD), lambda l, j: (l, 0, j)),
        out_shape=jax.ShapeDtypeStruct((depth, b, n), F32),
        compiler_params=pltpu.CompilerParams(
            dimension_semantics=("arbitrary", "arbitrary"), vmem_limit_bytes=VMEM_LIMIT),
        name="modulation",
    )(c, mod_w, mod_b.reshape(depth, 1, n))


def _premix_kernel(x_ref, mod_ref, g_ref, w_ref, bf_ref, pw_ref, ps_ref, tri_ref,
                   wo_ref, wup_ref, wdn_ref,
                   k_ref, fk_ref, qt_ref, vt_ref, ft_ref, pool_ref, wo_bf_ref, wup_bf_ref, wdn_bf_ref,
                   carry_ref, ubuf_ref, wku_ref, wt_ref, *, tm):
    i = pl.program_id(1)

    wo_bf_ref[...] = wo_ref[...].astype(BF16)
    wup_bf_ref[...] = wup_ref[...].astype(BF16)
    wdn_bf_ref[...] = wdn_ref[...].astype(BF16)

    @pl.when((pl.program_id(0) == 0) & (i == 0))
    def _():
        a0, a1, a2, a3 = ATTN_W, 2 * ATTN_W, 3 * ATTN_W, 3 * ATTN_W + N_HEADS
        wku_ref[:ATTN_W, :] = w_ref[a0:a1, :].astype(BF16)
        wku_ref[ATTN_W:, :] = w_ref[a3:, :].astype(BF16)
        wt_ref[0:ATTN_W, :] = (w_ref[0:a0, :] * (HEAD_DIM ** -0.5 * LOG2E)).astype(BF16)
        wt_ref[ATTN_W:2 * ATTN_W, :] = w_ref[a1:a2, :].astype(BF16)
        wt_ref[2 * ATTN_W:, :] = jnp.concatenate(
            [w_ref[a2:a3, :], jnp.zeros((WT_F_ROWS - N_HEADS, w_ref.shape[1]), F32)], axis=0).astype(BF16)

    @pl.when(i == 0)
    def _():
        carry_ref[...] = jnp.zeros_like(carry_ref)
        ubuf_ref[...] = jnp.zeros_like(ubuf_ref)

    x = x_ref[...]
    ms = jnp.mean(x * x, axis=-1, keepdims=True)
    scale = g_ref[...] * (1.0 + mod_ref[1:2, :])
    h = (x * lax.rsqrt(ms + EPS)) * scale + mod_ref[0:1, :]
    hb = h.astype(BF16)

    ku = lax.dot_general(hb, wku_ref[...], (((1,), (1,)), ((), ())),
                         preferred_element_type=F32)
    k_ref[...] = ku[:, :ATTN_W].astype(BF16)
    u = ku[:, ATTN_W:]

    zt = lax.dot_general(wt_ref[...], hb, (((1,), (1,)), ((), ())),
                         preferred_element_type=F32)
    qt_ref[...] = zt[0:ATTN_W].astype(BF16)
    vt_ref[...] = zt[ATTN_W:2 * ATTN_W].astype(BF16)

    fl = zt[2 * ATTN_W:2 * ATTN_W + N_HEADS] + bf_ref[...]
    logf = jnp.minimum(fl, 0.0) - jnp.log1p(jnp.exp(-jnp.abs(fl)))
    pieces = jnp.concatenate(list(_split3(logf)) + [jnp.zeros((SUBLANES, tm), F32)], axis=0)
    cs = jnp.dot(pieces.astype(BF16), tri_ref[...], preferred_element_type=F32)
    f_cum = (cs[0:8] + cs[8:16] + cs[16:24]) + carry_ref[:, 0:1]
    carry_ref[...] = jnp.broadcast_to(f_cum[:, tm - 1:tm], carry_ref.shape)
    f_log2 = f_cum * LOG2E
    ft_ref[...] = f_log2

    blk = jnp.concatenate(list(_split3(-f_log2))
                          + [jnp.zeros((LANES - F_PIECES * N_HEADS, tm), F32)], axis=0)
    blk_t = blk.T
    lane = lax.broadcasted_iota(jnp.int32, (tm, LANES), 1)
    blk_t = jnp.where((lane >= ONES_LANE) & (lane < ONES_LANE + F_PIECES), 1.0, blk_t)
    fk_ref[...] = blk_t.astype(BF16)

    pos = i * tm + lax.broadcasted_iota(jnp.int32, (tm, POOL_C), 0)
    for g, w in enumerate(POOL_WINDOWS):
        c0 = g * POOL_C
        ug = u[:, c0:c0 + POOL_C]
        ws = jnp.concatenate([ubuf_ref[:, c0:c0 + POOL_C], ug], axis=0)
        shift = 1
        while shift < w:
            ws = ws + pltpu.roll(ws, shift, axis=0)
            shift *= 2
        cnt = jnp.minimum(pos + 1, w).astype(F32)
        d = ws[HALO:] / cnt - ug
        pg = jnp.dot(d.astype(BF16), pw_ref[g], preferred_element_type=F32)
        pool_ref[:, c0:c0 + POOL_C] = (pg * ps_ref[:, c0:c0 + POOL_C]).astype(BF16)
    ubuf_ref[...] = u[tm - HALO:, :]


def _premix(x, mod, g1, w_in_t, layer, b_f, pool_w, pool_scale, tri, w_out, ffn_up, ffn_down):
    b, s, d = x.shape
    tm = TM_PREMIX
    once = pl.Buffered(1)
    n_i = s // tm
    n_steps = b * n_i

    def cast_specs(w):
        rows, cols = w.shape[1:]
        share = 1
        while rows % (n_steps // share) or (rows // (n_steps // share)) % BF16_SUBLANES:
            share *= 2
        slab = rows // (n_steps // share)
        return (pl.BlockSpec((None, slab, cols), lambda bi, i: (layer, (bi * n_i + i) // share, 0)),
                pl.BlockSpec((slab, cols), lambda bi, i: ((bi * n_i + i) // share, 0)),
                jax.ShapeDtypeStruct((rows, cols), BF16))

    cast_in, cast_out, cast_shape = zip(*(cast_specs(w) for w in (w_out, ffn_up, ffn_down)))
    kern = functools.partial(_premix_kernel, tm=tm)
    row = lambda bi, i: (bi, i, 0)
    col = lambda bi, i: (bi, 0, i)
    const2 = lambda bi, i: (0, 0)
    return pl.pallas_call(
        kern,
        grid=(b, s // tm),
        in_specs=[
            pl.BlockSpec((None, tm, d), row),
            pl.BlockSpec((None, 6, d), lambda bi, i: (bi, 0, 0)),
            pl.BlockSpec((1, d), const2),
            pl.BlockSpec((None, w_in_t.shape[1], d), lambda bi, i: (layer, 0, 0), pipeline_mode=once),
            pl.BlockSpec((N_HEADS, 1), const2),
            pl.BlockSpec((len(POOL_WINDOWS), POOL_C, POOL_C), lambda bi, i: (0, 0, 0)),
            pl.BlockSpec((1, POOL_W), const2),
            pl.BlockSpec((tm, tm), const2),
            *cast_in,
        ],
        out_specs=[
            pl.BlockSpec((None, tm, ATTN_W), row),
            pl.BlockSpec((None, tm, LANES), row),
            pl.BlockSpec((None, ATTN_W, tm), col),
            pl.BlockSpec((None, ATTN_W, tm), col),
            pl.BlockSpec((None, N_HEADS, tm), col),
            pl.BlockSpec((None, tm, POOL_W), row),
            *cast_out,
        ],
        out_shape=[
            jax.ShapeDtypeStruct((b, s, ATTN_W), BF16),
            jax.ShapeDtypeStruct((b, s, LANES), BF16),
            jax.ShapeDtypeStruct((b, ATTN_W, s), BF16),
            jax.ShapeDtypeStruct((b, ATTN_W, s), BF16),
            jax.ShapeDtypeStruct((b, N_HEADS, s), F32),
            jax.ShapeDtypeStruct((b, s, POOL_W), BF16),
            *cast_shape,
        ],
        scratch_shapes=[
            pltpu.VMEM((N_HEADS, LANES), F32),
            pltpu.VMEM((HALO, POOL_W), F32),
            pltpu.VMEM((ATTN_W + POOL_W, d), BF16),
            pltpu.VMEM((2 * ATTN_W + WT_F_ROWS, d), BF16),
        ],
        compiler_params=pltpu.CompilerParams(
            dimension_semantics=("arbitrary", "arbitrary"), vmem_limit_bytes=VMEM_LIMIT),
        name="premix",
    )(x, mod, g1, w_in_t, b_f, pool_w, pool_scale, tri, w_out, ffn_up, ffn_down)


def _attn_kernel(k_ref, fk_ref, qt_ref, vt_ref, ft_ref, o_ref, s_scr, p_scr, *, tq, tk, seq):
    assert tq == 2 * tk
    p = pl.program_id(1)
    row = lax.broadcasted_iota(jnp.int32, (LANES, tq), 0)
    key_pos = lax.broadcasted_iota(jnp.int32, (tk, tq), 0)
    qry_pos = lax.broadcasted_iota(jnp.int32, (tk, tq), 1)
    ones_rows = jnp.ones((V_ONES_ROWS, tk), BF16)

    def scores(t, q_all):
        r0 = t * tk
        k_aug = jnp.concatenate([k_ref[r0:r0 + tk, :], fk_ref[r0:r0 + tk, :]], axis=1)
        s_scr[t % 2] = jnp.dot(k_aug, q_all, preferred_element_type=F32)

    def softmax(t, ms, mask):
        new_ms, alphas = [], []
        for hh in range(2):
            s = s_scr[t % 2, :, hh * tq:(hh + 1) * tq]
            if mask is not None:
                s = jnp.where(mask, s, -jnp.inf)
            m_new = jnp.maximum(ms[hh], jnp.max(s, axis=0, keepdims=True))
            alphas.append(jnp.exp2(ms[hh] - m_new))
            new_ms.append(m_new)
            p_scr[t % 2, :, hh * tq:(hh + 1) * tq] = jnp.exp2(s - m_new).astype(BF16)
        return new_ms, alphas

    def accumulate(t, alphas, accs):
        r0 = t * tk
        out = []
        for hh in range(2):
            v_aug = jnp.concatenate(
                [vt_ref[hh * HEAD_DIM:(hh + 1) * HEAD_DIM, r0:r0 + tk], ones_rows], axis=0)
            pv = jnp.dot(v_aug, p_scr[t % 2, :, hh * tq:(hh + 1) * tq],
                         preferred_element_type=F32)
            out.append(alphas[hh] * accs[hh] + pv)
        return out

    for qb in range(seq // tq):
        c0 = qb * tq
        qt2 = qt_ref[:, c0:c0 + tq]
        q_augs = []
        for hh in range(2):
            head = 2 * p + hh
            q_h = jnp.where((row >= hh * HEAD_DIM) & (row < (hh + 1) * HEAD_DIM), qt2,
                            jnp.zeros_like(qt2))
            pieces = _split3(ft_ref[hh:hh + 1, c0:c0 + tq])
            fblk = jnp.where((row < F_PIECES * N_HEADS) & (row % N_HEADS == head), 1.0, 0.0)
            for j in range(F_PIECES):
                fblk = jnp.where(row == ONES_LANE + j, pieces[j], fblk)
            q_augs.append(jnp.concatenate([q_h, fblk.astype(BF16)], axis=0))
        q_all = jnp.concatenate(q_augs, axis=1)

        n_full = 2 * qb
        n_tiles = n_full + 2
        ms = [jnp.full((1, tq), NEG_BIG, F32) for _ in range(2)]
        accs = [jnp.zeros((HEAD_DIM + V_ONES_ROWS, tq), F32) for _ in range(2)]
        pending = None
        scores(0, q_all)
        for t in range(n_tiles):
            if t + 1 < n_tiles:
                scores(t + 1, q_all)
            mask = None if t < n_full else (key_pos + (t - n_full) * tk <= qry_pos)
            ms, alphas = softmax(t, ms, mask)
            if pending is not None:
                accs = accumulate(*pending, accs)
            pending = (t, alphas)
        accs = accumulate(*pending, accs)
        out_t = jnp.concatenate(
            [a[:HEAD_DIM] / a[HEAD_DIM:HEAD_DIM + 1] for a in accs], axis=0)
        o_ref[c0:c0 + tq, :] = out_t.T.astype(BF16)


def _attention(k, fk, qt, vt, ft):
    b, s, _ = k.shape
    n_pairs = N_HEADS // 2
    kern = functools.partial(_attn_kernel, tq=TQ_ATTN, tk=TK_ATTN, seq=s)
    return pl.pallas_call(
        kern,
        grid=(b, n_pairs),
        in_specs=[
            pl.BlockSpec((None, s, LANES), lambda bi, p: (bi, 0, p)),
            pl.BlockSpec((None, s, LANES), lambda bi, p: (bi, 0, 0)),
            pl.BlockSpec((None, LANES, s), lambda bi, p: (bi, p, 0)),
            pl.BlockSpec((None, LANES, s), lambda bi, p: (bi, p, 0)),
            pl.BlockSpec((None, None, 2, s), lambda bi, p: (bi, p, 0, 0)),
        ],
        out_specs=pl.BlockSpec((None, s, LANES), lambda bi, p: (bi, 0, p)),
        out_shape=jax.ShapeDtypeStruct((b, s, ATTN_W), BF16),
        scratch_shapes=[
            pltpu.VMEM((2, TK_ATTN, 2 * TQ_ATTN), F32),
            pltpu.VMEM((2, TK_ATTN, 2 * TQ_ATTN), BF16),
        ],
        compiler_params=pltpu.CompilerParams(
            dimension_semantics=("arbitrary", "arbitrary"), vmem_limit_bytes=VMEM_LIMIT),
        name="fox_attention",
    )(k, fk, qt, vt, ft.reshape(b, n_pairs, 2, s))


def _ffn_kernel(x_ref, attn_ref, pool_ref, mod_ref, g_ref, wo_ref, wup_ref, cw_ref, cb_ref,
                wdn_ref, fg_ref, o_ref, h_ref, act_ref, carry_ref, *, tm, final):
    i = pl.program_id(1)

    @pl.when(i == 0)
    def _():
        carry_ref[...] = jnp.zeros_like(carry_ref)

    mixed = jnp.concatenate([attn_ref[...], pool_ref[...]], axis=1)
    y = jnp.dot(mixed, wo_ref[...], preferred_element_type=F32)
    x1 = x_ref[...] + mod_ref[2:3, :] * y
    o_ref[...] = x1
    ms = jnp.mean(x1 * x1, axis=-1, keepdims=True)
    scale = g_ref[...] * (1.0 + mod_ref[4:5, :])
    h_ref[...] = ((x1 * lax.rsqrt(ms + EPS)) * scale + mod_ref[3:4, :]).astype(BF16)

    def conv_branch(c0):
        cols = slice(c0, c0 + FF_CHUNK)
        a = jnp.dot(h_ref[...], wup_ref[:, cols], preferred_element_type=F32)
        ext = jnp.concatenate([carry_ref[:, cols], a], axis=0)
        carry_ref[:, cols] = a[tm - SUBLANES:tm, :]
        a1 = pltpu.roll(ext, 1, axis=0)[SUBLANES:]
        a2 = pltpu.roll(ext, 2, axis=0)[SUBLANES:]
        return (cb_ref[:, cols] + a2 * cw_ref[0:1, cols] + a1 * cw_ref[1:2, cols]
                + a * cw_ref[2:3, cols])

    for c in range(N_FF_CHUNKS):
        gate = conv_branch(c * FF_CHUNK)
        val = conv_branch(D_FF + c * FF_CHUNK)
        act_ref[:, c * FF_CHUNK:(c + 1) * FF_CHUNK] = (gate * jax.nn.sigmoid(gate) * val).astype(BF16)

    y2 = jnp.dot(act_ref[...], wdn_ref[...], preferred_element_type=F32)
    x2 = o_ref[...] + mod_ref[5:6, :] * y2
    if final:
        ms2 = jnp.mean(x2 * x2, axis=-1, keepdims=True)
        x2 = (x2 * lax.rsqrt(ms2 + EPS)) * fg_ref[...]
    o_ref[...] = x2


def _ffn(x, attn, pool, mod, g2, wo, wup, cw, cb, wdn, fg, layer, final):
    b, s, d = x.shape
    tm = TM_FFN
    kern = functools.partial(_ffn_kernel, tm=tm, final=final)
    row = lambda bi, i: (bi, i, 0)
    const2 = lambda bi, i: (0, 0)
    per_layer = lambda bi, i: (layer, 0, 0)
    once = pl.Buffered(1)
    return pl.pallas_call(
        kern,
        grid=(b, s // tm),
        in_specs=[
            pl.BlockSpec((None, tm, d), row),
            pl.BlockSpec((None, tm, ATTN_W), row),
            pl.BlockSpec((None, tm, POOL_W), row),
            pl.BlockSpec((None, 6, d), lambda bi, i: (bi, 0, 0)),
            pl.BlockSpec((1, d), const2),
            pl.BlockSpec((d, d), const2, pipeline_mode=once),
            pl.BlockSpec((d, 2 * D_FF), const2, pipeline_mode=once),
            pl.BlockSpec((None, CONV_W, 2 * D_FF), per_layer),
            pl.BlockSpec((None, 1, 2 * D_FF), per_layer),
            pl.BlockSpec((D_FF, d), const2, pipeline_mode=once),
            pl.BlockSpec((1, d), const2),
        ],
        out_specs=pl.BlockSpec((None, tm, d), row),
        out_shape=jax.ShapeDtypeStruct((b, s, d), F32),
        scratch_shapes=[
            pltpu.VMEM((tm, d), BF16),
            pltpu.VMEM((tm, D_FF), BF16),
            pltpu.VMEM((SUBLANES, 2 * D_FF), F32),
        ],
        compiler_params=pltpu.CompilerParams(
            dimension_semantics=("arbitrary", "arbitrary"), vmem_limit_bytes=VMEM_LIMIT),
        name="outproj_convffn",
    )(x, attn, pool, mod, g2, wo, wup, cw, cb, wdn, fg)


def kernel(x, c, mod_w, mod_b, norm1_g, norm2_g, w_in, b_f, pool_w, pool_scale, w_out, ffn_up,
           ffn_conv_w, ffn_conv_b, ffn_down, final_g):
    b, s, d = x.shape
    depth = mod_w.shape[0]
    assert (d, s % TM_PREMIX, s % TQ_ATTN, s % TM_FFN) == (D_MODEL, 0, 0, 0)

    mod = _modulation(c, mod_w, mod_b).reshape(depth, b, 6, d)
    tri = (lax.broadcasted_iota(jnp.int32, (TM_PREMIX, TM_PREMIX), 0)
           <= lax.broadcasted_iota(jnp.int32, (TM_PREMIX, TM_PREMIX), 1)).astype(BF16)
    fg = final_g.reshape(1, d)
    cb = ffn_conv_b.reshape(depth, 1, 2 * D_FF)
    w_in_t = jnp.swapaxes(w_in, 1, 2)

    for l in range(depth):
        k, fk, qt, vt, ft, pool, wo, wup, wdn = _premix(
            x, mod[l], norm1_g[l].reshape(1, d), w_in_t, l, b_f[l].reshape(N_HEADS, 1),
            pool_w[l].astype(BF16), pool_scale[l].reshape(1, POOL_W), tri, w_out, ffn_up, ffn_down)
        attn = _attention(k, fk, qt, vt, ft)
        x = _ffn(x, attn, pool, mod[l], norm2_g[l].reshape(1, d), wo, wup, ffn_conv_w, cb, wdn, fg,
                 layer=l, final=(l == depth - 1))
    return x
```

```python
import functools

import jax
import jax.numpy as jnp
from jax import lax
from jax.experimental import pallas as pl
from jax.experimental.pallas import tpu as pltpu

F32 = jnp.float32
BF16 = jnp.bfloat16

D_MODEL = 1024
N_HEADS = 8
HEAD_DIM = 64
ATTN_W = N_HEADS * HEAD_DIM
POOL_WINDOWS = (2, 4, 8, 16)
POOL_C = 128
POOL_W = POOL_C * len(POOL_WINDOWS)
D_FF = 2816
CONV_W = 3
N_MOD = 6
EPS = 1e-6

LANES = 128
SUBLANES = 8
BF16_SUBLANES = 16
HALO = 16
FF_CHUNK = 256
N_FF_CHUNKS = D_FF // FF_CHUNK
F_PIECES = 3
WT_F_ROWS = 16
ONES_LANE = 32
NEG_BIG = -1e30
LOG2E = 1.4426950408889634
V_ONES_ROWS = 16

TM_PREMIX = 512
TQ_ATTN = 512
TK_ATTN = 256
TM_FFN = 512
VMEM_LIMIT = 56 * 1024 * 1024


def _split3(x):
    hi = x.astype(BF16).astype(F32)
    r = x - hi
    mid = r.astype(BF16).astype(F32)
    lo = (r - mid).astype(BF16).astype(F32)
    return hi, mid, lo


def _mod_kernel(c_ref, w_ref, b_ref, o_ref):
    c = c_ref[...]
    ca = (c * jax.nn.sigmoid(c)).astype(BF16)
    o_ref[...] = jnp.dot(ca, w_ref[...].astype(BF16), preferred_element_type=F32) + b_ref[...]


def _modulation(c, mod_w, mod_b):
    depth, d, n = mod_w.shape
    b = c.shape[0]
    return pl.pallas_call(
        _mod_kernel,
        grid=(depth, n // d),
        in_specs=[
            pl.BlockSpec((b, d), lambda l, j: (0, 0)),
            pl.BlockSpec((None, d, d), lambda l, j: (l, 0, j)),
            pl.BlockSpec((None, 1, d), lambda l, j: (l, 0, j)),
        ],
        out_specs=pl.BlockSpec((None, None, b, d), lambda l, j: (l, j, 0, 0)),
        out_shape=jax.ShapeDtypeStruct((depth, n // d, b, d), F32),
        compiler_params=pltpu.CompilerParams(
            dimension_semantics=("arbitrary", "arbitrary"), vmem_limit_bytes=VMEM_LIMIT),
        name="modulation",
    )(c, mod_w, mod_b.reshape(depth, 1, n))


def _premix_kernel(x_ref, mod_ref, g_ref, w_ref, bf_ref, pw_ref, ps_ref, tri_ref,
                   wo_ref, wup_ref, wdn_ref,
                   k_ref, fk_ref, qt_ref, vt_ref, ft_ref, pool_ref, wo_bf_ref, wup_bf_ref, wdn_bf_ref,
                   carry_ref, ubuf_ref, wku_ref, wt_ref, *, tm):
    i = pl.program_id(1)

    wo_bf_ref[...] = wo_ref[...].astype(BF16)
    wup_bf_ref[...] = wup_ref[...].astype(BF16)
    wdn_bf_ref[...] = wdn_ref[...].astype(BF16)

    @pl.when((pl.program_id(0) == 0) & (i == 0))
    def _():
        a0, a1, a2, a3 = ATTN_W, 2 * ATTN_W, 3 * ATTN_W, 3 * ATTN_W + N_HEADS
        wku_ref[:ATTN_W, :] = w_ref[a0:a1, :].astype(BF16)
        wku_ref[ATTN_W:, :] = w_ref[a3:, :].astype(BF16)
        wt_ref[0:ATTN_W, :] = (w_ref[0:a0, :] * (HEAD_DIM ** -0.5 * LOG2E)).astype(BF16)
        wt_ref[ATTN_W:2 * ATTN_W, :] = w_ref[a1:a2, :].astype(BF16)
        wt_ref[2 * ATTN_W:, :] = jnp.concatenate(
            [w_ref[a2:a3, :], jnp.zeros((WT_F_ROWS - N_HEADS, w_ref.shape[1]), F32)], axis=0).astype(BF16)

    @pl.when(i == 0)
    def _():
        carry_ref[...] = jnp.zeros_like(carry_ref)
        ubuf_ref[...] = jnp.zeros_like(ubuf_ref)

    x = x_ref[...]
    ms = jnp.mean(x * x, axis=-1, keepdims=True)
    seq = pl.ds(pl.program_id(0), 1)
    scale = g_ref[...] * (1.0 + mod_ref[1, seq, :])
    h = (x * lax.rsqrt(ms + EPS)) * scale + mod_ref[0, seq, :]
    hb = h.astype(BF16)

    ku = lax.dot_general(hb, wku_ref[...], (((1,), (1,)), ((), ())),
                         preferred_element_type=F32)
    k_ref[...] = ku[:, :ATTN_W].astype(BF16)
    u = ku[:, ATTN_W:]

    zt = lax.dot_general(wt_ref[...], hb, (((1,), (1,)), ((), ())),
                         preferred_element_type=F32)
    qt_ref[...] = zt[0:ATTN_W].astype(BF16)
    vt_ref[...] = zt[ATTN_W:2 * ATTN_W].astype(BF16)

    fl = zt[2 * ATTN_W:2 * ATTN_W + N_HEADS] + bf_ref[...]
    logf = jnp.minimum(fl, 0.0) - jnp.log1p(jnp.exp(-jnp.abs(fl)))
    pieces = jnp.concatenate(list(_split3(logf)) + [jnp.zeros((SUBLANES, tm), F32)], axis=0)
    cs = jnp.dot(pieces.astype(BF16), tri_ref[...], preferred_element_type=F32)
    f_cum = (cs[0:8] + cs[8:16] + cs[16:24]) + carry_ref[:, 0:1]
    carry_ref[...] = jnp.broadcast_to(f_cum[:, tm - 1:tm], carry_ref.shape)
    f_log2 = f_cum * LOG2E
    ft_ref[...] = f_log2

    blk = jnp.concatenate(list(_split3(-f_log2))
                          + [jnp.zeros((LANES - F_PIECES * N_HEADS, tm), F32)], axis=0)
    blk_t = blk.T
    lane = lax.broadcasted_iota(jnp.int32, (tm, LANES), 1)
    blk_t = jnp.where((lane >= ONES_LANE) & (lane < ONES_LANE + F_PIECES), 1.0, blk_t)
    fk_ref[...] = blk_t.astype(BF16)

    pos = i * tm + lax.broadcasted_iota(jnp.int32, (tm, POOL_C), 0)
    for g, w in enumerate(POOL_WINDOWS):
        c0 = g * POOL_C
        ug = u[:, c0:c0 + POOL_C]
        ws = jnp.concatenate([ubuf_ref[:, c0:c0 + POOL_C], ug], axis=0)
        shift = 1
        while shift < w:
            ws = ws + pltpu.roll(ws, shift, axis=0)
            shift *= 2
        cnt = jnp.minimum(pos + 1, w).astype(F32)
        d = ws[HALO:] / cnt - ug
        pg = jnp.dot(d.astype(BF16), pw_ref[g], preferred_element_type=F32)
        pool_ref[:, c0:c0 + POOL_C] = (pg * ps_ref[:, c0:c0 + POOL_C]).astype(BF16)
    ubuf_ref[...] = u[tm - HALO:, :]


def _premix(x, mod, g1, w_in_t, layer, b_f, pool_w, pool_scale, tri, w_out, ffn_up, ffn_down):
    b, s, d = x.shape
    tm = TM_PREMIX
    once = pl.Buffered(1)
    n_i = s // tm
    n_steps = b * n_i

    def cast_specs(w):
        rows, cols = w.shape[1:]
        share = 1
        while rows % (n_steps // share) or (rows // (n_steps // share)) % BF16_SUBLANES:
            share *= 2
        slab = rows // (n_steps // share)
        return (pl.BlockSpec((None, slab, cols), lambda bi, i: (layer, (bi * n_i + i) // share, 0)),
                pl.BlockSpec((slab, cols), lambda bi, i: ((bi * n_i + i) // share, 0)),
                jax.ShapeDtypeStruct((rows, cols), BF16))

    cast_in, cast_out, cast_shape = zip(*(cast_specs(w) for w in (w_out, ffn_up, ffn_down)))
    kern = functools.partial(_premix_kernel, tm=tm)
    row = lambda bi, i: (bi, i, 0)
    col = lambda bi, i: (bi, 0, i)
    const2 = lambda bi, i: (0, 0)
    return pl.pallas_call(
        kern,
        grid=(b, s // tm),
        in_specs=[
            pl.BlockSpec((None, tm, d), row),
            pl.BlockSpec((N_MOD, b, d), lambda bi, i: (0, 0, 0)),
            pl.BlockSpec((1, d), const2),
            pl.BlockSpec((None, w_in_t.shape[1], d), lambda bi, i: (layer, 0, 0), pipeline_mode=once),
            pl.BlockSpec((N_HEADS, 1), const2),
            pl.BlockSpec(pool_w.shape, lambda bi, i: (0, 0, 0)),
            pl.BlockSpec((1, POOL_W), const2),
            pl.BlockSpec((tm, tm), const2),
            *cast_in,
        ],
        out_specs=[
            pl.BlockSpec((None, tm, ATTN_W), row),
            pl.BlockSpec((None, tm, LANES), row),
            pl.BlockSpec((None, ATTN_W, tm), col),
            pl.BlockSpec((None, ATTN_W, tm), col),
            pl.BlockSpec((None, N_HEADS, tm), col),
            pl.BlockSpec((None, tm, POOL_W), row),
            *cast_out,
        ],
        out_shape=[
            jax.ShapeDtypeStruct((b, s, ATTN_W), BF16),
            jax.ShapeDtypeStruct((b, s, LANES), BF16),
            jax.ShapeDtypeStruct((b, ATTN_W, s), BF16),
            jax.ShapeDtypeStruct((b, ATTN_W, s), BF16),
            jax.ShapeDtypeStruct((b, N_HEADS, s), F32),
            jax.ShapeDtypeStruct((b, s, POOL_W), BF16),
            *cast_shape,
        ],
        scratch_shapes=[
            pltpu.VMEM((N_HEADS, LANES), F32),
            pltpu.VMEM((HALO, POOL_W), F32),
            pltpu.VMEM((ATTN_W + POOL_W, d), BF16),
            pltpu.VMEM((2 * ATTN_W + WT_F_ROWS, d), BF16),
        ],
        compiler_params=pltpu.CompilerParams(
            dimension_semantics=("arbitrary", "arbitrary"), vmem_limit_bytes=VMEM_LIMIT),
        name="premix",
    )(x, mod, g1, w_in_t, b_f, pool_w, pool_scale, tri, w_out, ffn_up, ffn_down)


def _attn_kernel(k_ref, fk_ref, qt_ref, vt_ref, ft_ref, o_ref, s_scr, p_scr, *, tq, tk, seq):
    assert tq == 2 * tk
    p = pl.program_id(1)
    row = lax.broadcasted_iota(jnp.int32, (LANES, tq), 0)
    key_pos = lax.broadcasted_iota(jnp.int32, (tk, tq), 0)
    qry_pos = lax.broadcasted_iota(jnp.int32, (tk, tq), 1)
    ones_rows = jnp.ones((V_ONES_ROWS, tk), BF16)

    def scores(t, q_all):
        r0 = t * tk
        k_aug = jnp.concatenate([k_ref[r0:r0 + tk, :], fk_ref[r0:r0 + tk, :]], axis=1)
        s_scr[t % 2] = jnp.dot(k_aug, q_all, preferred_element_type=F32)

    def softmax(t, ms, mask):
        new_ms, alphas = [], []
        for hh in range(2):
            s = s_scr[t % 2, :, hh * tq:(hh + 1) * tq]
            if mask is not None:
                s = jnp.where(mask, s, -jnp.inf)
            m_new = jnp.maximum(ms[hh], jnp.max(s, axis=0, keepdims=True))
            alphas.append(jnp.exp2(ms[hh] - m_new))
            new_ms.append(m_new)
            p_scr[t % 2, :, hh * tq:(hh + 1) * tq] = jnp.exp2(s - m_new).astype(BF16)
        return new_ms, alphas

    def accumulate(t, alphas, accs):
        r0 = t * tk
        out = []
        for hh in range(2):
            v_aug = jnp.concatenate(
                [vt_ref[hh * HEAD_DIM:(hh + 1) * HEAD_DIM, r0:r0 + tk], ones_rows], axis=0)
            pv = jnp.dot(v_aug, p_scr[t % 2, :, hh * tq:(hh + 1) * tq],
                         preferred_element_type=F32)
            out.append(alphas[hh] * accs[hh] + pv)
        return out

    for qb in range(seq // tq):
        c0 = qb * tq
        qt2 = qt_ref[:, c0:c0 + tq]
        q_augs = []
        for hh in range(2):
            head = 2 * p + hh
            q_h = jnp.where((row >= hh * HEAD_DIM) & (row < (hh + 1) * HEAD_DIM), qt2,
                            jnp.zeros_like(qt2))
            pieces = _split3(ft_ref[hh:hh + 1, c0:c0 + tq])
            fblk = jnp.where((row < F_PIECES * N_HEADS) & (row % N_HEADS == head), 1.0, 0.0)
            for j in range(F_PIECES):
                fblk = jnp.where(row == ONES_LANE + j, pieces[j], fblk)
            q_augs.append(jnp.concatenate([q_h, fblk.astype(BF16)], axis=0))
        q_all = jnp.concatenate(q_augs, axis=1)

        n_full = 2 * qb
        n_tiles = n_full + 2
        ms = [jnp.full((1, tq), NEG_BIG, F32) for _ in range(2)]
        accs = [jnp.zeros((HEAD_DIM + V_ONES_ROWS, tq), F32) for _ in range(2)]
        pending = None
        scores(0, q_all)
        for t in range(n_tiles):
            if t + 1 < n_tiles:
                scores(t + 1, q_all)
            mask = None if t < n_full else (key_pos + (t - n_full) * tk <= qry_pos)
            ms, alphas = softmax(t, ms, mask)
            if pending is not None:
                accs = accumulate(*pending, accs)
            pending = (t, alphas)
        accs = accumulate(*pending, accs)
        out_t = jnp.concatenate(
            [a[:HEAD_DIM] / a[HEAD_DIM:HEAD_DIM + 1] for a in accs], axis=0)
        o_ref[c0:c0 + tq, :] = out_t.T.astype(BF16)


def _attention(k, fk, qt, vt, ft):
    b, s, _ = k.shape
    n_pairs = N_HEADS // 2
    kern = functools.partial(_attn_kernel, tq=TQ_ATTN, tk=TK_ATTN, seq=s)
    return pl.pallas_call(
        kern,
        grid=(b, n_pairs),
        in_specs=[
            pl.BlockSpec((None, s, LANES), lambda bi, p: (bi, 0, p)),
            pl.BlockSpec((None, s, LANES), lambda bi, p: (bi, 0, 0)),
            pl.BlockSpec((None, LANES, s), lambda bi, p: (bi, p, 0)),
            pl.BlockSpec((None, LANES, s), lambda bi, p: (bi, p, 0)),
            pl.BlockSpec((None, None, 2, s), lambda bi, p: (bi, p, 0, 0)),
        ],
        out_specs=pl.BlockSpec((None, s, LANES), lambda bi, p: (bi, 0, p)),
        out_shape=jax.ShapeDtypeStruct((b, s, ATTN_W), BF16),
        scratch_shapes=[
            pltpu.VMEM((2, TK_ATTN, 2 * TQ_ATTN), F32),
            pltpu.VMEM((2, TK_ATTN, 2 * TQ_ATTN), BF16),
        ],
        compiler_params=pltpu.CompilerParams(
            dimension_semantics=("arbitrary", "arbitrary"), vmem_limit_bytes=VMEM_LIMIT),
        name="fox_attention",
    )(k, fk, qt, vt, ft.reshape(b, n_pairs, 2, s))


def _ffn_kernel(x_ref, attn_ref, pool_ref, mod_ref, g_ref, wo_ref, wup_ref, cw_ref, cb_ref,
                wdn_ref, fg_ref, o_ref, h_ref, act_ref, carry_ref, *, tm, final):
    i = pl.program_id(1)

    @pl.when(i == 0)
    def _():
        carry_ref[...] = jnp.zeros_like(carry_ref)

    mixed = jnp.concatenate([attn_ref[...], pool_ref[...]], axis=1)
    y = jnp.dot(mixed, wo_ref[...], preferred_element_type=F32)
    seq = pl.ds(pl.program_id(0), 1)
    x1 = x_ref[...] + mod_ref[2, seq, :] * y
    o_ref[...] = x1
    ms = jnp.mean(x1 * x1, axis=-1, keepdims=True)
    scale = g_ref[...] * (1.0 + mod_ref[4, seq, :])
    h_ref[...] = ((x1 * lax.rsqrt(ms + EPS)) * scale + mod_ref[3, seq, :]).astype(BF16)

    def conv_branch(c0):
        cols = slice(c0, c0 + FF_CHUNK)
        a = jnp.dot(h_ref[...], wup_ref[:, cols], preferred_element_type=F32)
        ext = jnp.concatenate([carry_ref[:, cols], a], axis=0)
        carry_ref[:, cols] = a[tm - SUBLANES:tm, :]
        a1 = pltpu.roll(ext, 1, axis=0)[SUBLANES:]
        a2 = pltpu.roll(ext, 2, axis=0)[SUBLANES:]
        return (cb_ref[:, cols] + a2 * cw_ref[0:1, cols] + a1 * cw_ref[1:2, cols]
                + a * cw_ref[2:3, cols])

    for c in range(N_FF_CHUNKS):
        gate = conv_branch(c * FF_CHUNK)
        val = conv_branch(D_FF + c * FF_CHUNK)
        half_gv = (0.5 * gate) * val
        act_ref[:, c * FF_CHUNK:(c + 1) * FF_CHUNK] = (
            half_gv + half_gv * jnp.tanh(0.5 * gate)).astype(BF16)

    y2 = jnp.dot(act_ref[...], wdn_ref[...], preferred_element_type=F32)
    x2 = o_ref[...] + mod_ref[5, seq, :] * y2
    if final:
        ms2 = jnp.mean(x2 * x2, axis=-1, keepdims=True)
        x2 = (x2 * lax.rsqrt(ms2 + EPS)) * fg_ref[...]
    o_ref[...] = x2


def _ffn(x, attn, pool, mod, g2, wo, wup, cw, cb, wdn, fg, layer, final):
    b, s, d = x.shape
    tm = TM_FFN
    kern = functools.partial(_ffn_kernel, tm=tm, final=final)
    row = lambda bi, i: (bi, i, 0)
    const2 = lambda bi, i: (0, 0)
    per_layer = lambda bi, i: (layer, 0, 0)
    once = pl.Buffered(1)
    return pl.pallas_call(
        kern,
        grid=(b, s // tm),
        in_specs=[
            pl.BlockSpec((None, tm, d), row),
            pl.BlockSpec((None, tm, ATTN_W), row),
            pl.BlockSpec((None, tm, POOL_W), row),
            pl.BlockSpec((N_MOD, b, d), lambda bi, i: (0, 0, 0)),
            pl.BlockSpec((1, d), const2),
            pl.BlockSpec((d, d), const2, pipeline_mode=once),
            pl.BlockSpec((d, 2 * D_FF), const2, pipeline_mode=once),
            pl.BlockSpec((None, CONV_W, 2 * D_FF), per_layer),
            pl.BlockSpec((None, 1, 2 * D_FF), per_layer),
            pl.BlockSpec((D_FF, d), const2, pipeline_mode=once),
            pl.BlockSpec((1, d), const2),
        ],
        out_specs=pl.BlockSpec((None, tm, d), row),
        out_shape=jax.ShapeDtypeStruct((b, s, d), F32),
        scratch_shapes=[
            pltpu.VMEM((tm, d), BF16),
            pltpu.VMEM((tm, D_FF), BF16),
            pltpu.VMEM((SUBLANES, 2 * D_FF), F32),
        ],
        compiler_params=pltpu.CompilerParams(
            dimension_semantics=("arbitrary", "arbitrary"), vmem_limit_bytes=VMEM_LIMIT),
        name="outproj_convffn",
    )(x, attn, pool, mod, g2, wo, wup, cw, cb, wdn, fg)


def kernel(x, c, mod_w, mod_b, norm1_g, norm2_g, w_in, b_f, pool_w, pool_scale, w_out, ffn_up,
           ffn_conv_w, ffn_conv_b, ffn_down, final_g):
    b, s, d = x.shape
    depth = mod_w.shape[0]
    assert (d, s % TM_PREMIX, s % TQ_ATTN, s % TM_FFN) == (D_MODEL, 0, 0, 0)

    mod = _modulation(c, mod_w, mod_b)
    tri = (lax.broadcasted_iota(jnp.int32, (TM_PREMIX, TM_PREMIX), 0)
           <= lax.broadcasted_iota(jnp.int32, (TM_PREMIX, TM_PREMIX), 1)).astype(BF16)
    fg = final_g.reshape(1, d)
    cb = ffn_conv_b.reshape(depth, 1, 2 * D_FF)
    w_in_t = jnp.swapaxes(w_in, 1, 2)

    for l in range(depth):
        k, fk, qt, vt, ft, pool, wo, wup, wdn = _premix(
            x, mod[l], norm1_g[l].reshape(1, d), w_in_t, l, b_f[l].reshape(N_HEADS, 1),
            pool_w[l].astype(BF16), pool_scale[l].reshape(1, POOL_W), tri, w_out, ffn_up, ffn_down)
        attn = _attention(k, fk, qt, vt, ft)
        x = _ffn(x, attn, pool, mod[l], norm2_g[l].reshape(1, d), wo, wup, ffn_conv_w, cb, wdn, fg,
                 layer=l, final=(l == depth - 1))
    return x
```

```python
import functools

import jax
import jax.numpy as jnp
from jax import lax
from jax.experimental import pallas as pl
from jax.experimental.pallas import tpu as pltpu

F32 = jnp.float32
BF16 = jnp.bfloat16

D_MODEL = 1024
N_HEADS = 8
HEAD_DIM = 64
ATTN_W = N_HEADS * HEAD_DIM
POOL_WINDOWS = (2, 4, 8, 16)
POOL_C = 128
POOL_W = POOL_C * len(POOL_WINDOWS)
D_FF = 2816
CONV_W = 3
N_MOD = 6
EPS = 1e-6

LANES = 128
SUBLANES = 8
BF16_SUBLANES = 16
HALO = 16
FF_CHUNK = 256
N_FF_CHUNKS = D_FF // FF_CHUNK
F_PIECES = 3
WT_F_ROWS = 16
ONES_LANE = 32
NEG_BIG = -1e30
LOG2E = 1.4426950408889634
V_ONES_ROWS = 16

TM_PREMIX = 512
TQ_ATTN = 512
TK_ATTN = 256
TM_FFN = 512
VMEM_LIMIT = 56 * 1024 * 1024


def _split3(x):
    hi = x.astype(BF16).astype(F32)
    r = x - hi
    mid = r.astype(BF16).astype(F32)
    lo = (r - mid).astype(BF16).astype(F32)
    return hi, mid, lo


def _mod_kernel(c_ref, w_ref, b_ref, o_ref):
    c = c_ref[...]
    ca = (c * jax.nn.sigmoid(c)).astype(BF16)
    o_ref[...] = jnp.dot(ca, w_ref[...].astype(BF16), preferred_element_type=F32) + b_ref[...]


def _modulation(c, mod_w, mod_b):
    depth, d, n = mod_w.shape
    b = c.shape[0]
    return pl.pallas_call(
        _mod_kernel,
        grid=(depth, n // d),
        in_specs=[
            pl.BlockSpec((b, d), lambda l, j: (0, 0)),
            pl.BlockSpec((None, d, d), lambda l, j: (l, 0, j)),
            pl.BlockSpec((None, 1, d), lambda l, j: (l, 0, j)),
        ],
        out_specs=pl.BlockSpec((None, None, b, d), lambda l, j: (l, j, 0, 0)),
        out_shape=jax.ShapeDtypeStruct((depth, n // d, b, d), F32),
        compiler_params=pltpu.CompilerParams(
            dimension_semantics=("arbitrary", "arbitrary"), vmem_limit_bytes=VMEM_LIMIT),
        name="modulation",
    )(c, mod_w, mod_b.reshape(depth, 1, n))


def _premix_kernel(x_ref, mod_ref, g_ref, w_ref, bf_ref, pw_ref, ps_ref, tri_ref,
                   wo_ref, wup_ref, wdn_ref,
                   k_ref, fk_ref, qt_ref, vt_ref, ft_ref, pool_ref, wo_bf_ref, wup_bf_ref, wdn_bf_ref,
                   carry_ref, ubuf_ref, wku_ref, wt_ref, *, tm):
    i = pl.program_id(1)

    @pl.when((pl.program_id(0) == 0) & (i == 0))
    def _():
        a0, a1, a2, a3 = ATTN_W, 2 * ATTN_W, 3 * ATTN_W, 3 * ATTN_W + N_HEADS
        wku_ref[:ATTN_W, :] = w_ref[a0:a1, :].astype(BF16)
        wku_ref[ATTN_W:, :] = w_ref[a3:, :].astype(BF16)
        wt_ref[0:WT_F_ROWS, :] = jnp.concatenate(
            [w_ref[a2:a3, :], jnp.zeros((WT_F_ROWS - N_HEADS, w_ref.shape[1]), F32)], axis=0).astype(BF16)
        wt_ref[WT_F_ROWS:WT_F_ROWS + ATTN_W, :] = (
            w_ref[0:a0, :] * (HEAD_DIM ** -0.5 * LOG2E)).astype(BF16)
        wt_ref[WT_F_ROWS + ATTN_W:, :] = w_ref[a1:a2, :].astype(BF16)

    @pl.when(i == 0)
    def _():
        carry_ref[...] = jnp.zeros_like(carry_ref)
        ubuf_ref[...] = jnp.zeros_like(ubuf_ref)

    x = x_ref[...]
    ms = jnp.mean(x * x, axis=-1, keepdims=True)
    seq = pl.ds(pl.program_id(0), 1)
    scale = g_ref[...] * (1.0 + mod_ref[1, seq, :])
    h = (x * lax.rsqrt(ms + EPS)) * scale + mod_ref[0, seq, :]
    hb = h.astype(BF16)

    ku = lax.dot_general(hb, wku_ref[...], (((1,), (1,)), ((), ())),
                         preferred_element_type=F32)
    k_ref[...] = ku[:, :ATTN_W].astype(BF16)
    u = ku[:, ATTN_W:]

    zt = lax.dot_general(wt_ref[...], hb, (((1,), (1,)), ((), ())),
                         preferred_element_type=F32)
    qt_ref[...] = zt[WT_F_ROWS:WT_F_ROWS + ATTN_W].astype(BF16)
    vt_ref[...] = zt[WT_F_ROWS + ATTN_W:].astype(BF16)

    wo_bf_ref[...] = wo_ref[...].astype(BF16)
    wup_bf_ref[...] = wup_ref[...].astype(BF16)
    wdn_bf_ref[...] = wdn_ref[...].astype(BF16)

    fl = zt[0:N_HEADS] + bf_ref[...]
    logf = jnp.minimum(fl, 0.0) - jnp.log1p(jnp.exp(-jnp.abs(fl)))
    pieces = jnp.concatenate(list(_split3(logf)) + [jnp.zeros((SUBLANES, tm), F32)], axis=0)
    cs = jnp.dot(pieces.astype(BF16), tri_ref[...], preferred_element_type=F32)
    f_cum = (cs[0:8] + cs[8:16] + cs[16:24]) + carry_ref[:, 0:1]
    carry_ref[...] = jnp.broadcast_to(f_cum[:, tm - 1:tm], carry_ref.shape)
    f_log2 = f_cum * LOG2E
    for pair in range(N_HEADS // 2):
        ft_ref[pair] = f_log2[2 * pair:2 * pair + 2]

    blk = jnp.concatenate(list(_split3(-f_log2))
                          + [jnp.zeros((LANES - F_PIECES * N_HEADS, tm), F32)], axis=0)
    blk_t = blk.T
    lane = lax.broadcasted_iota(jnp.int32, (tm, LANES), 1)
    blk_t = jnp.where((lane >= ONES_LANE) & (lane < ONES_LANE + F_PIECES), 1.0, blk_t)
    fk_ref[...] = blk_t.astype(BF16)

    pos = i * tm + lax.broadcasted_iota(jnp.int32, (tm, POOL_C), 0)
    for g, w in enumerate(POOL_WINDOWS):
        c0 = g * POOL_C
        ug = u[:, c0:c0 + POOL_C]
        ws = jnp.concatenate([ubuf_ref[:, c0:c0 + POOL_C], ug], axis=0)
        shift = 1
        while shift < w:
            ws = ws + pltpu.roll(ws, shift, axis=0)
            shift *= 2
        cnt = jnp.minimum(pos + 1, w).astype(F32)
        d = ws[HALO:] / cnt - ug
        pg = jnp.dot(d.astype(BF16), pw_ref[g], preferred_element_type=F32)
        pool_ref[:, c0:c0 + POOL_C] = (pg * ps_ref[:, c0:c0 + POOL_C]).astype(BF16)
    ubuf_ref[...] = u[tm - HALO:, :]


def _premix(x, mod, g1, w_in_t, layer, b_f, pool_w, pool_scale, tri, w_out, ffn_up, ffn_down):
    b, s, d = x.shape
    tm = TM_PREMIX
    once = pl.Buffered(1)
    n_i = s // tm
    n_steps = b * n_i

    def cast_specs(w):
        rows, cols = w.shape[1:]
        share = 1
        while rows % (n_steps // share) or (rows // (n_steps // share)) % BF16_SUBLANES:
            share *= 2
        slab = rows // (n_steps // share)
        return (pl.BlockSpec((None, slab, cols), lambda bi, i: (layer, (bi * n_i + i) // share, 0)),
                pl.BlockSpec((slab, cols), lambda bi, i: ((bi * n_i + i) // share, 0)),
                jax.ShapeDtypeStruct((rows, cols), BF16))

    cast_in, cast_out, cast_shape = zip(*(cast_specs(w) for w in (w_out, ffn_up, ffn_down)))
    kern = functools.partial(_premix_kernel, tm=tm)
    row = lambda bi, i: (bi, i, 0)
    col = lambda bi, i: (bi, 0, i)
    const2 = lambda bi, i: (0, 0)
    per_layer = lambda bi, i: (layer, 0, 0)
    return pl.pallas_call(
        kern,
        grid=(b, s // tm),
        in_specs=[
            pl.BlockSpec((None, tm, d), row),
            pl.BlockSpec((N_MOD, b, d), lambda bi, i: (0, 0, 0)),
            pl.BlockSpec((None, 1, d), per_layer),
            pl.BlockSpec((None, w_in_t.shape[1], d), per_layer, pipeline_mode=once),
            pl.BlockSpec((None, N_HEADS, 1), per_layer),
            pl.BlockSpec((None,) + pool_w.shape[1:], lambda bi, i: (layer, 0, 0, 0)),
            pl.BlockSpec((None, 1, POOL_W), per_layer),
            pl.BlockSpec((tm, tm), const2),
            *cast_in,
        ],
        out_specs=[
            pl.BlockSpec((None, tm, ATTN_W), row),
            pl.BlockSpec((None, tm, LANES), row),
            pl.BlockSpec((None, ATTN_W, tm), col),
            pl.BlockSpec((None, ATTN_W, tm), col),
            pl.BlockSpec((None, N_HEADS // 2, 2, tm), lambda bi, i: (bi, 0, 0, i)),
            pl.BlockSpec((None, tm, POOL_W), row),
            *cast_out,
        ],
        out_shape=[
            jax.ShapeDtypeStruct((b, s, ATTN_W), BF16),
            jax.ShapeDtypeStruct((b, s, LANES), BF16),
            jax.ShapeDtypeStruct((b, ATTN_W, s), BF16),
            jax.ShapeDtypeStruct((b, ATTN_W, s), BF16),
            jax.ShapeDtypeStruct((b, N_HEADS // 2, 2, s), F32),
            jax.ShapeDtypeStruct((b, s, POOL_W), BF16),
            *cast_shape,
        ],
        scratch_shapes=[
            pltpu.VMEM((N_HEADS, LANES), F32),
            pltpu.VMEM((HALO, POOL_W), F32),
            pltpu.VMEM((ATTN_W + POOL_W, d), BF16),
            pltpu.VMEM((2 * ATTN_W + WT_F_ROWS, d), BF16),
        ],
        compiler_params=pltpu.CompilerParams(
            dimension_semantics=("arbitrary", "arbitrary"), vmem_limit_bytes=VMEM_LIMIT),
        name="premix",
    )(x, mod, g1, w_in_t, b_f, pool_w, pool_scale, tri, w_out, ffn_up, ffn_down)


def _attn_kernel(k_ref, fk_ref, qt_ref, vt_ref, ft_ref, o_ref, s_scr, p_scr, *, tq, tk, seq):
    assert tq % tk == 0
    p = pl.program_id(1)
    row = lax.broadcasted_iota(jnp.int32, (LANES, tq), 0)
    key_pos = lax.broadcasted_iota(jnp.int32, (tk, tq), 0)
    qry_pos = lax.broadcasted_iota(jnp.int32, (tk, tq), 1)
    ones_rows = jnp.ones((V_ONES_ROWS, tk), BF16)

    def scores(t, q_all):
        r0 = t * tk
        k_aug = jnp.concatenate([k_ref[r0:r0 + tk, :], fk_ref[r0:r0 + tk, :]], axis=1)
        s_scr[t % 2] = jnp.dot(k_aug, q_all, preferred_element_type=F32)

    def softmax(t, ms, mask):
        new_ms, alphas = [], []
        for hh in range(2):
            s = s_scr[t % 2, :, hh * tq:(hh + 1) * tq]
            if mask is not None:
                s = jnp.where(mask, s, -jnp.inf)
            m_new = jnp.maximum(ms[hh], jnp.max(s, axis=0, keepdims=True))
            alphas.append(jnp.exp2(ms[hh] - m_new))
            new_ms.append(m_new)
            p_scr[t % 2, :, hh * tq:(hh + 1) * tq] = jnp.exp2(s - m_new).astype(BF16)
        return new_ms, alphas

    def accumulate(t, alphas, accs):
        r0 = t * tk
        out = []
        for hh in range(2):
            v_aug = jnp.concatenate(
                [vt_ref[hh * HEAD_DIM:(hh + 1) * HEAD_DIM, r0:r0 + tk], ones_rows], axis=0)
            pv = jnp.dot(v_aug, p_scr[t % 2, :, hh * tq:(hh + 1) * tq],
                         preferred_element_type=F32)
            out.append(alphas[hh] * accs[hh] + pv)
        return out

    for qb in range(seq // tq):
        c0 = qb * tq
        qt2 = qt_ref[:, c0:c0 + tq]
        q_augs = []
        for hh in range(2):
            head = 2 * p + hh
            q_h = jnp.where((row >= hh * HEAD_DIM) & (row < (hh + 1) * HEAD_DIM), qt2,
                            jnp.zeros_like(qt2))
            pieces = _split3(ft_ref[hh:hh + 1, c0:c0 + tq])
            fblk = jnp.where((row < F_PIECES * N_HEADS) & (row % N_HEADS == head), 1.0, 0.0)
            for j in range(F_PIECES):
                fblk = jnp.where(row == ONES_LANE + j, pieces[j], fblk)
            q_augs.append(jnp.concatenate([q_h, fblk.astype(BF16)], axis=0))
        q_all = jnp.concatenate(q_augs, axis=1)

        n_full = (tq // tk) * qb
        n_tiles = n_full + tq // tk
        ms = [jnp.full((1, tq), NEG_BIG, F32) for _ in range(2)]
        accs = [jnp.zeros((HEAD_DIM + V_ONES_ROWS, tq), F32) for _ in range(2)]
        pending = None
        scores(0, q_all)
        for t in range(n_tiles):
            if t + 1 < n_tiles:
                scores(t + 1, q_all)
            mask = None if t < n_full else (key_pos + (t - n_full) * tk <= qry_pos)
            ms, alphas = softmax(t, ms, mask)
            if pending is not None:
                accs = accumulate(*pending, accs)
            pending = (t, alphas)
        accs = accumulate(*pending, accs)
        out_t = jnp.concatenate(
            [a[:HEAD_DIM] / a[HEAD_DIM:HEAD_DIM + 1] for a in accs], axis=0)
        o_ref[c0:c0 + tq, :] = out_t.T.astype(BF16)


def _attention(k, fk, qt, vt, ft):
    b, s, _ = k.shape
    n_pairs = N_HEADS // 2
    kern = functools.partial(_attn_kernel, tq=TQ_ATTN, tk=TK_ATTN, seq=s)
    return pl.pallas_call(
        kern,
        grid=(b, n_pairs),
        in_specs=[
            pl.BlockSpec((None, s, LANES), lambda bi, p: (bi, 0, p)),
            pl.BlockSpec((None, s, LANES), lambda bi, p: (bi, 0, 0)),
            pl.BlockSpec((None, LANES, s), lambda bi, p: (bi, p, 0)),
            pl.BlockSpec((None, LANES, s), lambda bi, p: (bi, p, 0)),
            pl.BlockSpec((None, None, 2, s), lambda bi, p: (bi, p, 0, 0)),
        ],
        out_specs=pl.BlockSpec((None, s, LANES), lambda bi, p: (bi, 0, p)),
        out_shape=jax.ShapeDtypeStruct((b, s, ATTN_W), BF16),
        scratch_shapes=[
            pltpu.VMEM((2, TK_ATTN, 2 * TQ_ATTN), F32),
            pltpu.VMEM((2, TK_ATTN, 2 * TQ_ATTN), BF16),
        ],
        compiler_params=pltpu.CompilerParams(
            dimension_semantics=("arbitrary", "arbitrary"), vmem_limit_bytes=VMEM_LIMIT),
        name="fox_attention",
    )(k, fk, qt, vt, ft)


def _ffn_kernel(x_ref, attn_ref, pool_ref, mod_ref, g_ref, wo_ref, wup_ref, cw_ref, cb_ref,
                wdn_ref, fg_ref, o_ref, h_ref, act_ref, carry_ref, *, tm, final):
    i = pl.program_id(1)

    @pl.when(i == 0)
    def _():
        carry_ref[...] = jnp.zeros_like(carry_ref)

    mixed = jnp.concatenate([attn_ref[...], pool_ref[...]], axis=1)
    y = jnp.dot(mixed, wo_ref[...], preferred_element_type=F32)
    seq = pl.ds(pl.program_id(0), 1)
    x1 = x_ref[...] + mod_ref[2, seq, :] * y
    o_ref[...] = x1
    ms = jnp.mean(x1 * x1, axis=-1, keepdims=True)
    scale = g_ref[...] * (1.0 + mod_ref[4, seq, :])
    h_ref[...] = ((x1 * lax.rsqrt(ms + EPS)) * scale + mod_ref[3, seq, :]).astype(BF16)

    def conv_branch(c0):
        cols = slice(c0, c0 + FF_CHUNK)
        a = jnp.dot(h_ref[...], wup_ref[:, cols], preferred_element_type=F32)
        ext = jnp.concatenate([carry_ref[:, cols], a], axis=0)
        carry_ref[:, cols] = a[tm - SUBLANES:tm, :]
        a1 = pltpu.roll(ext, 1, axis=0)[SUBLANES:]
        a2 = pltpu.roll(ext, 2, axis=0)[SUBLANES:]
        return (cb_ref[:, cols] + a2 * cw_ref[0:1, cols] + a1 * cw_ref[1:2, cols]
                + a * cw_ref[2:3, cols])

    for c in range(N_FF_CHUNKS):
        gate = conv_branch(c * FF_CHUNK)
        val = conv_branch(D_FF + c * FF_CHUNK)
        half_gv = (0.5 * gate) * val
        act_ref[:, c * FF_CHUNK:(c + 1) * FF_CHUNK] = (
            half_gv + half_gv * jnp.tanh(0.5 * gate)).astype(BF16)

    y2 = jnp.dot(act_ref[...], wdn_ref[...], preferred_element_type=F32)
    x2 = o_ref[...] + mod_ref[5, seq, :] * y2
    if final:
        ms2 = jnp.mean(x2 * x2, axis=-1, keepdims=True)
        x2 = (x2 * lax.rsqrt(ms2 + EPS)) * fg_ref[...]
    o_ref[...] = x2


def _ffn(x, attn, pool, mod, g2, wo, wup, cw, cb, wdn, fg, layer, final):
    b, s, d = x.shape
    tm = TM_FFN
    kern = functools.partial(_ffn_kernel, tm=tm, final=final)
    row = lambda bi, i: (bi, i, 0)
    const2 = lambda bi, i: (0, 0)
    per_layer = lambda bi, i: (layer, 0, 0)
    once = pl.Buffered(1)
    return pl.pallas_call(
        kern,
        grid=(b, s // tm),
        in_specs=[
            pl.BlockSpec((None, tm, d), row),
            pl.BlockSpec((None, tm, ATTN_W), row),
            pl.BlockSpec((None, tm, POOL_W), row),
            pl.BlockSpec((N_MOD, b, d), lambda bi, i: (0, 0, 0)),
            pl.BlockSpec((None, 1, d), per_layer),
            pl.BlockSpec((d, d), const2, pipeline_mode=once),
            pl.BlockSpec((d, 2 * D_FF), const2, pipeline_mode=once),
            pl.BlockSpec((None, CONV_W, 2 * D_FF), per_layer),
            pl.BlockSpec((None, 1, 2 * D_FF), per_layer),
            pl.BlockSpec((D_FF, d), const2, pipeline_mode=once),
            pl.BlockSpec((1, d), const2),
        ],
        out_specs=pl.BlockSpec((None, tm, d), row),
        out_shape=jax.ShapeDtypeStruct((b, s, d), F32),
        scratch_shapes=[
            pltpu.VMEM((tm, d), BF16),
            pltpu.VMEM((tm, D_FF), BF16),
            pltpu.VMEM((SUBLANES, 2 * D_FF), F32),
        ],
        compiler_params=pltpu.CompilerParams(
            dimension_semantics=("arbitrary", "arbitrary"), vmem_limit_bytes=VMEM_LIMIT),
        name="outproj_convffn",
    )(x, attn, pool, mod, g2, wo, wup, cw, cb, wdn, fg)


def kernel(x, c, mod_w, mod_b, norm1_g, norm2_g, w_in, b_f, pool_w, pool_scale, w_out, ffn_up,
           ffn_conv_w, ffn_conv_b, ffn_down, final_g):
    b, s, d = x.shape
    depth = mod_w.shape[0]
    assert (d, s % TM_PREMIX, s % TQ_ATTN, s % TM_FFN) == (D_MODEL, 0, 0, 0)

    mod = _modulation(c, mod_w, mod_b)
    tri = (lax.broadcasted_iota(jnp.int32, (TM_PREMIX, TM_PREMIX), 0)
           <= lax.broadcasted_iota(jnp.int32, (TM_PREMIX, TM_PREMIX), 1)).astype(BF16)
    fg = final_g.reshape(1, d)
    g1, g2 = norm1_g.reshape(depth, 1, d), norm2_g.reshape(depth, 1, d)
    cb = ffn_conv_b.reshape(depth, 1, 2 * D_FF)
    bf = b_f.reshape(depth, N_HEADS, 1)
    ps = pool_scale.reshape(depth, 1, POOL_W)
    pw = pool_w.astype(BF16)
    w_in_t = jnp.swapaxes(w_in, 1, 2)

    for l in range(depth):
        k, fk, qt, vt, ft, pool, wo, wup, wdn = _premix(
            x, mod[l], g1, w_in_t, l, bf, pw, ps, tri, w_out, ffn_up, ffn_down)
        attn = _attention(k, fk, qt, vt, ft)
        x = _ffn(x, attn, pool, mod[l], g2, wo, wup, ffn_conv_w, cb, wdn, fg,
                 layer=l, final=(l == depth - 1))
    return x
```

```python
import functools

import jax
import jax.numpy as jnp
from jax import lax
from jax.experimental import pallas as pl
from jax.experimental.pallas import tpu as pltpu

F32 = jnp.float32
BF16 = jnp.bfloat16

D_MODEL = 1024
N_HEADS = 8
HEAD_DIM = 64
ATTN_W = N_HEADS * HEAD_DIM
POOL_WINDOWS = (2, 4, 8, 16)
POOL_C = 128
POOL_W = POOL_C * len(POOL_WINDOWS)
D_FF = 2816
CONV_W = 3
N_MOD = 6
EPS = 1e-6

LANES = 128
SUBLANES = 8
BF16_SUBLANES = 16
HALO = 16
FF_CHUNK = 256
N_FF_CHUNKS = D_FF // FF_CHUNK
F_PIECES = 3
WT_F_ROWS = 16
ONES_LANE = 32
NEG_BIG = -1e30
LOG2E = 1.4426950408889634
V_ONES_ROWS = 16

TM_PREMIX = 1024
TQ_ATTN = 512
TK_ATTN = 256
ATTN_PAIRS = 1
TM_FFN = 1024
VMEM_LIMIT = 56 * 1024 * 1024


def _split3(x):
    hi = x.astype(BF16).astype(F32)
    r = x - hi
    mid = r.astype(BF16).astype(F32)
    lo = (r - mid).astype(BF16).astype(F32)
    return hi, mid, lo


def _mod_kernel(c_ref, w_ref, b_ref, o_ref):
    c = c_ref[...]
    ca = (c * jax.nn.sigmoid(c)).astype(BF16)
    o_ref[...] = jnp.dot(ca, w_ref[...].astype(BF16), preferred_element_type=F32) + b_ref[...]


def _modulation(c, mod_w, mod_b):
    depth, d, n = mod_w.shape
    b = c.shape[0]
    return pl.pallas_call(
        _mod_kernel,
        grid=(depth, n // d),
        in_specs=[
            pl.BlockSpec((b, d), lambda l, j: (0, 0)),
            pl.BlockSpec((None, d, d), lambda l, j: (l, 0, j)),
            pl.BlockSpec((None, 1, d), lambda l, j: (l, 0, j)),
        ],
        out_specs=pl.BlockSpec((None, None, b, d), lambda l, j: (l, j, 0, 0)),
        out_shape=jax.ShapeDtypeStruct((depth, n // d, b, d), F32),
        compiler_params=pltpu.CompilerParams(
            dimension_semantics=("arbitrary", "arbitrary"), vmem_limit_bytes=VMEM_LIMIT),
        name="modulation",
    )(c, mod_w, mod_b.reshape(depth, 1, n))


def _premix_kernel(x_ref, mod_ref, g_ref, w_ref, bf_ref, pw_ref, ps_ref, tri_ref,
                   wo_ref, wup_ref, wdn_ref,
                   k_ref, fk_ref, qt_ref, vt_ref, ft_ref, pool_ref, wo_bf_ref, wup_bf_ref, wdn_bf_ref,
                   carry_ref, ubuf_ref, wku_ref, wt_ref, *, tm):
    i = pl.program_id(1)

    @pl.when((pl.program_id(0) == 0) & (i == 0))
    def _():
        a0, a1, a2, a3 = ATTN_W, 2 * ATTN_W, 3 * ATTN_W, 3 * ATTN_W + N_HEADS
        wku_ref[:ATTN_W, :] = w_ref[a0:a1, :].astype(BF16)
        wku_ref[ATTN_W:, :] = w_ref[a3:, :].astype(BF16)
        wt_ref[0:WT_F_ROWS, :] = jnp.concatenate(
            [w_ref[a2:a3, :], jnp.zeros((WT_F_ROWS - N_HEADS, w_ref.shape[1]), F32)], axis=0).astype(BF16)
        wt_ref[WT_F_ROWS:WT_F_ROWS + ATTN_W, :] = (
            w_ref[0:a0, :] * (HEAD_DIM ** -0.5 * LOG2E)).astype(BF16)
        wt_ref[WT_F_ROWS + ATTN_W:, :] = w_ref[a1:a2, :].astype(BF16)

    @pl.when(i == 0)
    def _():
        carry_ref[...] = jnp.zeros_like(carry_ref)
        ubuf_ref[...] = jnp.zeros_like(ubuf_ref)

    x = x_ref[...]
    ms = jnp.mean(x * x, axis=-1, keepdims=True)
    seq = pl.ds(pl.program_id(0), 1)
    scale = g_ref[...] * (1.0 + mod_ref[1, seq, :])
    h = (x * lax.rsqrt(ms + EPS)) * scale + mod_ref[0, seq, :]
    hb = h.astype(BF16)

    ku = lax.dot_general(hb, wku_ref[...], (((1,), (1,)), ((), ())),
                         preferred_element_type=F32)
    k_ref[...] = ku[:, :ATTN_W].astype(BF16)
    u = ku[:, ATTN_W:]

    zt = lax.dot_general(wt_ref[...], hb, (((1,), (1,)), ((), ())),
                         preferred_element_type=F32)
    qt_ref[...] = zt[WT_F_ROWS:WT_F_ROWS + ATTN_W].astype(BF16)
    vt_ref[...] = zt[WT_F_ROWS + ATTN_W:].astype(BF16)

    wo_bf_ref[...] = wo_ref[...].astype(BF16)
    wup_bf_ref[...] = wup_ref[...].astype(BF16)
    wdn_bf_ref[...] = wdn_ref[...].astype(BF16)

    fl = zt[0:N_HEADS] + bf_ref[...]
    logf = jnp.minimum(fl, 0.0) - jnp.log1p(jnp.exp(-jnp.abs(fl)))
    pieces = jnp.concatenate(list(_split3(logf)) + [jnp.zeros((SUBLANES, tm), F32)], axis=0)
    cs = jnp.dot(pieces.astype(BF16), tri_ref[...], preferred_element_type=F32)
    f_cum = (cs[0:8] + cs[8:16] + cs[16:24]) + carry_ref[:, 0:1]
    carry_ref[...] = jnp.broadcast_to(f_cum[:, tm - 1:tm], carry_ref.shape)
    f_log2 = f_cum * LOG2E
    for pair in range(N_HEADS // 2):
        ft_ref[pair] = f_log2[2 * pair:2 * pair + 2]

    blk = jnp.concatenate(list(_split3(-f_log2))
                          + [jnp.zeros((LANES - F_PIECES * N_HEADS, tm), F32)], axis=0)
    blk_t = blk.T
    lane = lax.broadcasted_iota(jnp.int32, (tm, LANES), 1)
    blk_t = jnp.where((lane >= ONES_LANE) & (lane < ONES_LANE + F_PIECES), 1.0, blk_t)
    fk_ref[...] = blk_t.astype(BF16)

    pos = i * tm + lax.broadcasted_iota(jnp.int32, (tm, POOL_C), 0)
    for g, w in enumerate(POOL_WINDOWS):
        c0 = g * POOL_C
        ug = u[:, c0:c0 + POOL_C]
        ws = jnp.concatenate([ubuf_ref[:, c0:c0 + POOL_C], ug], axis=0)
        shift = 1
        while shift < w:
            ws = ws + pltpu.roll(ws, shift, axis=0)
            shift *= 2
        cnt = jnp.minimum(pos + 1, w).astype(F32)
        d = ws[HALO:] / cnt - ug
        pg = jnp.dot(d.astype(BF16), pw_ref[g], preferred_element_type=F32)
        pool_ref[:, c0:c0 + POOL_C] = (pg * ps_ref[:, c0:c0 + POOL_C]).astype(BF16)
    ubuf_ref[...] = u[tm - HALO:, :]


def _premix(x, mod, g1, w_in_t, layer, b_f, pool_w, pool_scale, tri, w_out, ffn_up, ffn_down):
    b, s, d = x.shape
    tm = TM_PREMIX
    once = pl.Buffered(1)
    n_i = s // tm
    n_steps = b * n_i

    def cast_specs(w):
        rows, cols = w.shape[1:]
        share = 1
        while rows % (n_steps // share) or (rows // (n_steps // share)) % BF16_SUBLANES:
            share *= 2
        slab = rows // (n_steps // share)
        return (pl.BlockSpec((None, slab, cols), lambda bi, i: (layer, (bi * n_i + i) // share, 0)),
                pl.BlockSpec((slab, cols), lambda bi, i: ((bi * n_i + i) // share, 0)),
                jax.ShapeDtypeStruct((rows, cols), BF16))

    cast_in, cast_out, cast_shape = zip(*(cast_specs(w) for w in (w_out, ffn_up, ffn_down)))
    kern = functools.partial(_premix_kernel, tm=tm)
    row = lambda bi, i: (bi, i, 0)
    col = lambda bi, i: (bi, 0, i)
    const2 = lambda bi, i: (0, 0)
    per_layer = lambda bi, i: (layer, 0, 0)
    return pl.pallas_call(
        kern,
        grid=(b, s // tm),
        in_specs=[
            pl.BlockSpec((None, tm, d), row),
            pl.BlockSpec((N_MOD, b, d), lambda bi, i: (0, 0, 0)),
            pl.BlockSpec((None, 1, d), per_layer),
            pl.BlockSpec((None, w_in_t.shape[1], d), per_layer, pipeline_mode=once),
            pl.BlockSpec((None, N_HEADS, 1), per_layer),
            pl.BlockSpec((None,) + pool_w.shape[1:], lambda bi, i: (layer, 0, 0, 0)),
            pl.BlockSpec((None, 1, POOL_W), per_layer),
            pl.BlockSpec((tm, tm), const2),
            *cast_in,
        ],
        out_specs=[
            pl.BlockSpec((None, tm, ATTN_W), row),
            pl.BlockSpec((None, tm, LANES), row),
            pl.BlockSpec((None, ATTN_W, tm), col),
            pl.BlockSpec((None, ATTN_W, tm), col),
            pl.BlockSpec((None, N_HEADS // 2, 2, tm), lambda bi, i: (bi, 0, 0, i)),
            pl.BlockSpec((None, tm, POOL_W), row),
            *cast_out,
        ],
        out_shape=[
            jax.ShapeDtypeStruct((b, s, ATTN_W), BF16),
            jax.ShapeDtypeStruct((b, s, LANES), BF16),
            jax.ShapeDtypeStruct((b, ATTN_W, s), BF16),
            jax.ShapeDtypeStruct((b, ATTN_W, s), BF16),
            jax.ShapeDtypeStruct((b, N_HEADS // 2, 2, s), F32),
            jax.ShapeDtypeStruct((b, s, POOL_W), BF16),
            *cast_shape,
        ],
        scratch_shapes=[
            pltpu.VMEM((N_HEADS, LANES), F32),
            pltpu.VMEM((HALO, POOL_W), F32),
            pltpu.VMEM((ATTN_W + POOL_W, d), BF16),
            pltpu.VMEM((2 * ATTN_W + WT_F_ROWS, d), BF16),
        ],
        compiler_params=pltpu.CompilerParams(
            dimension_semantics=("arbitrary", "arbitrary"), vmem_limit_bytes=VMEM_LIMIT),
        name="premix",
    )(x, mod, g1, w_in_t, b_f, pool_w, pool_scale, tri, w_out, ffn_up, ffn_down)


def _attn_kernel(k_ref, fk_ref, qt_ref, vt_ref, ft_ref, o_ref, s_scr, p_scr, *, tq, tk, seq, pairs):
    assert tq % tk == 0
    first_pair = pl.program_id(1) * pairs
    row = lax.broadcasted_iota(jnp.int32, (LANES, tq), 0)
    key_pos = lax.broadcasted_iota(jnp.int32, (tk, tq), 0)
    qry_pos = lax.broadcasted_iota(jnp.int32, (tk, tq), 1)
    ones_rows = jnp.ones((V_ONES_ROWS, tk), BF16)

    def scores(pr, t, q_all):
        r0 = t * tk
        k_aug = jnp.concatenate([k_ref[r0:r0 + tk, pr * LANES:(pr + 1) * LANES],
                                 fk_ref[r0:r0 + tk, :]], axis=1)
        s_scr[pr, t % 2] = jnp.dot(k_aug, q_all, preferred_element_type=F32)

    def softmax(pr, t, ms, mask):
        new_ms, alphas = [], []
        for hh in range(2):
            s = s_scr[pr, t % 2, :, hh * tq:(hh + 1) * tq]
            if mask is not None:
                s = jnp.where(mask, s, -jnp.inf)
            m_new = jnp.maximum(ms[hh], jnp.max(s, axis=0, keepdims=True))
            alphas.append(jnp.exp2(ms[hh] - m_new))
            new_ms.append(m_new)
            p_scr[pr, t % 2, :, hh * tq:(hh + 1) * tq] = jnp.exp2(s - m_new).astype(BF16)
        return new_ms, alphas

    def accumulate(pr, t, alphas, accs):
        r0 = t * tk
        out = []
        for hh in range(2):
            v0 = pr * LANES + hh * HEAD_DIM
            v_aug = jnp.concatenate([vt_ref[v0:v0 + HEAD_DIM, r0:r0 + tk], ones_rows], axis=0)
            pv = jnp.dot(v_aug, p_scr[pr, t % 2, :, hh * tq:(hh + 1) * tq],
                         preferred_element_type=F32)
            out.append(alphas[hh] * accs[hh] + pv)
        return out

    def build_q(pr, c0):
        qt2 = qt_ref[pr * LANES:(pr + 1) * LANES, c0:c0 + tq]
        q_augs = []
        for hh in range(2):
            head = 2 * (first_pair + pr) + hh
            q_h = jnp.where((row >= hh * HEAD_DIM) & (row < (hh + 1) * HEAD_DIM), qt2,
                            jnp.zeros_like(qt2))
            pieces = _split3(ft_ref[pr, hh:hh + 1, c0:c0 + tq])
            fblk = jnp.where((row < F_PIECES * N_HEADS) & (row % N_HEADS == head), 1.0, 0.0)
            for j in range(F_PIECES):
                fblk = jnp.where(row == ONES_LANE + j, pieces[j], fblk)
            q_augs.append(jnp.concatenate([q_h, fblk.astype(BF16)], axis=0))
        return jnp.concatenate(q_augs, axis=1)

    prs = range(pairs)
    for qb in range(seq // tq):
        c0 = qb * tq
        q_all = [build_q(pr, c0) for pr in prs]
        n_full = (tq // tk) * qb
        n_tiles = n_full + tq // tk
        ms = [[jnp.full((1, tq), NEG_BIG, F32) for _ in range(2)] for _ in prs]
        accs = [[jnp.zeros((HEAD_DIM + V_ONES_ROWS, tq), F32) for _ in range(2)] for _ in prs]
        pending = None
        for pr in prs:
            scores(pr, 0, q_all[pr])
        for t in range(n_tiles):
            if t + 1 < n_tiles:
                for pr in prs:
                    scores(pr, t + 1, q_all[pr])
            mask = None if t < n_full else (key_pos + (t - n_full) * tk <= qry_pos)
            alphas = [None] * pairs
            for pr in prs:
                ms[pr], alphas[pr] = softmax(pr, t, ms[pr], mask)
            if pending is not None:
                for pr in prs:
                    accs[pr] = accumulate(pr, pending[0], pending[1][pr], accs[pr])
            pending = (t, alphas)
        for pr in prs:
            accs[pr] = accumulate(pr, pending[0], pending[1][pr], accs[pr])
            out_t = jnp.concatenate(
                [a[:HEAD_DIM] / a[HEAD_DIM:HEAD_DIM + 1] for a in accs[pr]], axis=0)
            o_ref[c0:c0 + tq, pr * LANES:(pr + 1) * LANES] = out_t.T.astype(BF16)


def _attention(k, fk, qt, vt, ft):
    b, s, _ = k.shape
    pp = ATTN_PAIRS
    w = pp * LANES
    kern = functools.partial(_attn_kernel, tq=TQ_ATTN, tk=TK_ATTN, seq=s, pairs=pp)
    return pl.pallas_call(
        kern,
        grid=(b, N_HEADS // 2 // pp),
        in_specs=[
            pl.BlockSpec((None, s, w), lambda bi, p: (bi, 0, p)),
            pl.BlockSpec((None, s, LANES), lambda bi, p: (bi, 0, 0)),
            pl.BlockSpec((None, w, s), lambda bi, p: (bi, p, 0)),
            pl.BlockSpec((None, w, s), lambda bi, p: (bi, p, 0)),
            pl.BlockSpec((None, pp, 2, s), lambda bi, p: (bi, p, 0, 0)),
        ],
        out_specs=pl.BlockSpec((None, s, w), lambda bi, p: (bi, 0, p)),
        out_shape=jax.ShapeDtypeStruct((b, s, ATTN_W), BF16),
        scratch_shapes=[
            pltpu.VMEM((pp, 2, TK_ATTN, 2 * TQ_ATTN), F32),
            pltpu.VMEM((pp, 2, TK_ATTN, 2 * TQ_ATTN), BF16),
        ],
        compiler_params=pltpu.CompilerParams(
            dimension_semantics=("arbitrary", "arbitrary"), vmem_limit_bytes=VMEM_LIMIT),
        name="fox_attention",
    )(k, fk, qt, vt, ft)


def _ffn_kernel(x_ref, attn_ref, pool_ref, mod_ref, g_ref, wo_ref, wup_ref, cw_ref, cb_ref,
                wdn_ref, fg_ref, o_ref, h_ref, act_ref, carry_ref, *, tm, final):
    i = pl.program_id(1)

    @pl.when(i == 0)
    def _():
        carry_ref[...] = jnp.zeros_like(carry_ref)

    mixed = jnp.concatenate([attn_ref[...], pool_ref[...]], axis=1)
    y = jnp.dot(mixed, wo_ref[...], preferred_element_type=F32)
    seq = pl.ds(pl.program_id(0), 1)
    x1 = x_ref[...] + mod_ref[2, seq, :] * y
    o_ref[...] = x1
    ms = jnp.mean(x1 * x1, axis=-1, keepdims=True)
    scale = g_ref[...] * (1.0 + mod_ref[4, seq, :])
    h_ref[...] = ((x1 * lax.rsqrt(ms + EPS)) * scale + mod_ref[3, seq, :]).astype(BF16)

    def conv_branch(c0):
        cols = slice(c0, c0 + FF_CHUNK)
        a = jnp.dot(h_ref[...], wup_ref[:, cols], preferred_element_type=F32)
        ext = jnp.concatenate([carry_ref[:, cols], a], axis=0)
        carry_ref[:, cols] = a[tm - SUBLANES:tm, :]
        a1 = pltpu.roll(ext, 1, axis=0)[SUBLANES:]
        a2 = pltpu.roll(ext, 2, axis=0)[SUBLANES:]
        return (cb_ref[:, cols] + a2 * cw_ref[0:1, cols] + a1 * cw_ref[1:2, cols]
                + a * cw_ref[2:3, cols])

    for c in range(N_FF_CHUNKS):
        gate = conv_branch(c * FF_CHUNK)
        val = conv_branch(D_FF + c * FF_CHUNK)
        half_gv = (0.5 * gate) * val
        act_ref[:, c * FF_CHUNK:(c + 1) * FF_CHUNK] = (
            half_gv + half_gv * jnp.tanh(0.5 * gate)).astype(BF16)

    y2 = jnp.dot(act_ref[...], wdn_ref[...], preferred_element_type=F32)
    x2 = o_ref[...] + mod_ref[5, seq, :] * y2
    if final:
        ms2 = jnp.mean(x2 * x2, axis=-1, keepdims=True)
        x2 = (x2 * lax.rsqrt(ms2 + EPS)) * fg_ref[...]
    o_ref[...] = x2


def _ffn(x, attn, pool, mod, g2, wo, wup, cw, cb, wdn, fg, layer, final):
    b, s, d = x.shape
    tm = TM_FFN
    kern = functools.partial(_ffn_kernel, tm=tm, final=final)
    row = lambda bi, i: (bi, i, 0)
    const2 = lambda bi, i: (0, 0)
    per_layer = lambda bi, i: (layer, 0, 0)
    once = pl.Buffered(1)
    return pl.pallas_call(
        kern,
        grid=(b, s // tm),
        in_specs=[
            pl.BlockSpec((None, tm, d), row),
            pl.BlockSpec((None, tm, ATTN_W), row),
            pl.BlockSpec((None, tm, POOL_W), row),
            pl.BlockSpec((N_MOD, b, d), lambda bi, i: (0, 0, 0)),
            pl.BlockSpec((None, 1, d), per_layer),
            pl.BlockSpec((d, d), const2, pipeline_mode=once),
            pl.BlockSpec((d, 2 * D_FF), const2, pipeline_mode=once),
            pl.BlockSpec((None, CONV_W, 2 * D_FF), per_layer),
            pl.BlockSpec((None, 1, 2 * D_FF), per_layer),
            pl.BlockSpec((D_FF, d), const2, pipeline_mode=once),
            pl.BlockSpec((1, d), const2),
        ],
        out_specs=pl.BlockSpec((None, tm, d), row),
        out_shape=jax.ShapeDtypeStruct((b, s, d), F32),
        scratch_shapes=[
            pltpu.VMEM((tm, d), BF16),
            pltpu.VMEM((tm, D_FF), BF16),
            pltpu.VMEM((SUBLANES, 2 * D_FF), F32),
        ],
        compiler_params=pltpu.CompilerParams(
            dimension_semantics=("arbitrary", "arbitrary"), vmem_limit_bytes=VMEM_LIMIT),
        name="outproj_convffn",
    )(x, attn, pool, mod, g2, wo, wup, cw, cb, wdn, fg)


def kernel(x, c, mod_w, mod_b, norm1_g, norm2_g, w_in, b_f, pool_w, pool_scale, w_out, ffn_up,
           ffn_conv_w, ffn_conv_b, ffn_down, final_g):
    b, s, d = x.shape
    depth = mod_w.shape[0]
    assert (d, s % TM_PREMIX, s % TQ_ATTN, s % TM_FFN) == (D_MODEL, 0, 0, 0)

    mod = _modulation(c, mod_w, mod_b)
    tri = (lax.broadcasted_iota(jnp.int32, (TM_PREMIX, TM_PREMIX), 0)
           <= lax.broadcasted_iota(jnp.int32, (TM_PREMIX, TM_PREMIX), 1)).astype(BF16)
    fg = final_g.reshape(1, d)
    g1, g2 = norm1_g.reshape(depth, 1, d), norm2_g.reshape(depth, 1, d)
    cb = ffn_conv_b.reshape(depth, 1, 2 * D_FF)
    bf = b_f.reshape(depth, N_HEADS, 1)
    ps = pool_scale.reshape(depth, 1, POOL_W)
    pw = pool_w.astype(BF16)
    w_in_t = jnp.swapaxes(w_in, 1, 2)

    for l in range(depth):
        k, fk, qt, vt, ft, pool, wo, wup, wdn = _premix(
            x, mod[l], g1, w_in_t, l, bf, pw, ps, tri, w_out, ffn_up, ffn_down)
        attn = _attention(k, fk, qt, vt, ft)
        x = _ffn(x, attn, pool, mod[l], g2, wo, wup, ffn_conv_w, cb, wdn, fg,
                 layer=l, final=(l == depth - 1))
    return x
```

```python
import functools

import jax
import jax.numpy as jnp
from jax import lax
from jax.experimental import pallas as pl
from jax.experimental.pallas import tpu as pltpu

F32 = jnp.float32
BF16 = jnp.bfloat16

D_MODEL = 1024
N_HEADS = 8
HEAD_DIM = 64
ATTN_W = N_HEADS * HEAD_DIM
POOL_WINDOWS = (2, 4, 8, 16)
POOL_C = 128
POOL_W = POOL_C * len(POOL_WINDOWS)
D_FF = 2816
CONV_W = 3
N_MOD = 6
EPS = 1e-6

LANES = 128
SUBLANES = 8
BF16_SUBLANES = 16
HALO = 16
TRI_ROWS = 256
FF_CHUNK = 256
N_FF_CHUNKS = D_FF // FF_CHUNK
F_PIECES = 3
WT_F_ROWS = 16
ONES_LANE = 32
NEG_BIG = -1e30
LOG2E = 1.4426950408889634
V_ONES_ROWS = 16

TM_PREMIX = 1024
TQ_ATTN = 512
TK_ATTN = 256
ATTN_PAIRS = 1
TM_FFN = 1024
VMEM_LIMIT = 56 * 1024 * 1024


def _split3(x):
    hi = x.astype(BF16).astype(F32)
    r = x - hi
    mid = r.astype(BF16).astype(F32)
    lo = (r - mid).astype(BF16).astype(F32)
    return hi, mid, lo


def _mod_kernel(c_ref, w_ref, b_ref, o_ref):
    c = c_ref[...]
    ca = (c * jax.nn.sigmoid(c)).astype(BF16)
    o_ref[...] = jnp.dot(ca, w_ref[...].astype(BF16), preferred_element_type=F32) + b_ref[...]


def _modulation(c, mod_w, mod_b):
    depth, d, n = mod_w.shape
    b = c.shape[0]
    return pl.pallas_call(
        _mod_kernel,
        grid=(depth, n // d),
        in_specs=[
            pl.BlockSpec((b, d), lambda l, j: (0, 0)),
            pl.BlockSpec((None, d, d), lambda l, j: (l, 0, j)),
            pl.BlockSpec((None, 1, d), lambda l, j: (l, 0, j)),
        ],
        out_specs=pl.BlockSpec((None, None, b, d), lambda l, j: (l, j, 0, 0)),
        out_shape=jax.ShapeDtypeStruct((depth, n // d, b, d), F32),
        compiler_params=pltpu.CompilerParams(
            dimension_semantics=("arbitrary", "arbitrary"), vmem_limit_bytes=VMEM_LIMIT),
        name="modulation",
    )(c, mod_w, mod_b.reshape(depth, 1, n))


def _premix_kernel(x_ref, mod_ref, g_ref, w_ref, bf_ref, pw_ref, ps_ref,
                   wo_ref, wup_ref, wdn_ref,
                   k_ref, fk_ref, qt_ref, vt_ref, ft_ref, pool_ref, wo_bf_ref, wup_bf_ref, wdn_bf_ref,
                   carry_ref, ubuf_ref, wku_ref, wt_ref, tri_ref, *, tm, layer):
    i = pl.program_id(1)

    @pl.when((pl.program_id(0) == 0) & (i == 0))
    def _():
        a0, a1, a2, a3 = ATTN_W, 2 * ATTN_W, 3 * ATTN_W, 3 * ATTN_W + N_HEADS
        wku_ref[:ATTN_W, :] = w_ref[a0:a1, :].astype(BF16)
        wku_ref[ATTN_W:, :] = w_ref[a3:, :].astype(BF16)
        wt_ref[0:WT_F_ROWS, :] = jnp.concatenate(
            [w_ref[a2:a3, :], jnp.zeros((WT_F_ROWS - N_HEADS, w_ref.shape[1]), F32)], axis=0).astype(BF16)
        wt_ref[WT_F_ROWS:WT_F_ROWS + ATTN_W, :] = (
            w_ref[0:a0, :] * (HEAD_DIM ** -0.5 * LOG2E)).astype(BF16)
        wt_ref[WT_F_ROWS + ATTN_W:, :] = w_ref[a1:a2, :].astype(BF16)
        for r0 in range(0, tm, TRI_ROWS):
            rr = r0 + lax.broadcasted_iota(jnp.int32, (TRI_ROWS, tm), 0)
            cc = lax.broadcasted_iota(jnp.int32, (TRI_ROWS, tm), 1)
            tri_ref[r0:r0 + TRI_ROWS, :] = jnp.where(rr <= cc, 1.0, 0.0).astype(BF16)

    @pl.when(i == 0)
    def _():
        carry_ref[...] = jnp.zeros_like(carry_ref)
        ubuf_ref[...] = jnp.zeros_like(ubuf_ref)

    x = x_ref[...]
    ms = jnp.mean(x * x, axis=-1, keepdims=True)
    seq = pl.ds(pl.program_id(0), 1)
    scale = g_ref[layer:layer + 1, :] * (1.0 + mod_ref[1, seq, :])
    h = (x * lax.rsqrt(ms + EPS)) * scale + mod_ref[0, seq, :]
    hb = h.astype(BF16)

    ku = lax.dot_general(hb, wku_ref[...], (((1,), (1,)), ((), ())),
                         preferred_element_type=F32)
    k_ref[...] = ku[:, :ATTN_W].astype(BF16)
    u = ku[:, ATTN_W:]

    zt = lax.dot_general(wt_ref[...], hb, (((1,), (1,)), ((), ())),
                         preferred_element_type=F32)
    qt_ref[...] = zt[WT_F_ROWS:WT_F_ROWS + ATTN_W].astype(BF16)
    vt_ref[...] = zt[WT_F_ROWS + ATTN_W:].astype(BF16)

    wo_bf_ref[...] = wo_ref[...].astype(BF16)
    wup_bf_ref[...] = wup_ref[...].astype(BF16)
    wdn_bf_ref[...] = wdn_ref[...].astype(BF16)

    fl = zt[0:N_HEADS] + bf_ref[...]
    logf = jnp.minimum(fl, 0.0) - jnp.log1p(jnp.exp(-jnp.abs(fl)))
    pieces = jnp.concatenate(list(_split3(logf)) + [jnp.zeros((SUBLANES, tm), F32)], axis=0)
    cs = jnp.dot(pieces.astype(BF16), tri_ref[...], preferred_element_type=F32)
    f_cum = (cs[0:8] + cs[8:16] + cs[16:24]) + carry_ref[:, 0:1]
    carry_ref[...] = jnp.broadcast_to(f_cum[:, tm - 1:tm], carry_ref.shape)
    f_log2 = f_cum * LOG2E
    for pair in range(N_HEADS // 2):
        ft_ref[pair] = f_log2[2 * pair:2 * pair + 2]

    blk = jnp.concatenate(list(_split3(-f_log2))
                          + [jnp.zeros((LANES - F_PIECES * N_HEADS, tm), F32)], axis=0)
    blk_t = blk.T
    lane = lax.broadcasted_iota(jnp.int32, (tm, LANES), 1)
    blk_t = jnp.where((lane >= ONES_LANE) & (lane < ONES_LANE + F_PIECES), 1.0, blk_t)
    fk_ref[...] = blk_t.astype(BF16)

    pos = i * tm + lax.broadcasted_iota(jnp.int32, (tm, POOL_C), 0)
    for g, w in enumerate(POOL_WINDOWS):
        c0 = g * POOL_C
        ug = u[:, c0:c0 + POOL_C]
        ws = jnp.concatenate([ubuf_ref[:, c0:c0 + POOL_C], ug], axis=0)
        shift = 1
        while shift < w:
            ws = ws + pltpu.roll(ws, shift, axis=0)
            shift *= 2
        cnt = jnp.minimum(pos + 1, w).astype(F32)
        d = ws[HALO:] / cnt - ug
        pg = jnp.dot(d.astype(BF16), pw_ref[g].astype(BF16), preferred_element_type=F32)
        pool_ref[:, c0:c0 + POOL_C] = (pg * ps_ref[layer:layer + 1, c0:c0 + POOL_C]).astype(BF16)
    ubuf_ref[...] = u[tm - HALO:, :]


def _premix(x, mod, g1, w_in_t, layer, b_f, pool_w, pool_scale, w_out, ffn_up, ffn_down):
    b, s, d = x.shape
    tm = TM_PREMIX
    once = pl.Buffered(1)
    n_i = s // tm
    n_steps = b * n_i

    def cast_specs(w):
        rows, cols = w.shape[1:]
        share = 1
        while rows % (n_steps // share) or (rows // (n_steps // share)) % BF16_SUBLANES:
            share *= 2
        slab = rows // (n_steps // share)
        return (pl.BlockSpec((None, slab, cols), lambda bi, i: (layer, (bi * n_i + i) // share, 0)),
                pl.BlockSpec((slab, cols), lambda bi, i: ((bi * n_i + i) // share, 0)),
                jax.ShapeDtypeStruct((rows, cols), BF16))

    cast_in, cast_out, cast_shape = zip(*(cast_specs(w) for w in (w_out, ffn_up, ffn_down)))
    kern = functools.partial(_premix_kernel, tm=tm, layer=layer)
    row = lambda bi, i: (bi, i, 0)
    col = lambda bi, i: (bi, 0, i)
    const2 = lambda bi, i: (0, 0)
    per_layer = lambda bi, i: (layer, 0, 0)
    return pl.pallas_call(
        kern,
        grid=(b, s // tm),
        in_specs=[
            pl.BlockSpec((None, tm, d), row),
            pl.BlockSpec((N_MOD, b, d), lambda bi, i: (0, 0, 0)),
            pl.BlockSpec(g1.shape, const2),
            pl.BlockSpec((None, w_in_t.shape[1], d), per_layer, pipeline_mode=once),
            pl.BlockSpec((None, N_HEADS, 1), per_layer),
            pl.BlockSpec((None,) + pool_w.shape[1:], lambda bi, i: (layer, 0, 0, 0)),
            pl.BlockSpec(pool_scale.shape, const2),
            *cast_in,
        ],
        out_specs=[
            pl.BlockSpec((None, tm, ATTN_W), row),
            pl.BlockSpec((None, tm, LANES), row),
            pl.BlockSpec((None, ATTN_W, tm), col),
            pl.BlockSpec((None, ATTN_W, tm), col),
            pl.BlockSpec((None, N_HEADS // 2, 2, tm), lambda bi, i: (bi, 0, 0, i)),
            pl.BlockSpec((None, tm, POOL_W), row),
            *cast_out,
        ],
        out_shape=[
            jax.ShapeDtypeStruct((b, s, ATTN_W), BF16),
            jax.ShapeDtypeStruct((b, s, LANES), BF16),
            jax.ShapeDtypeStruct((b, ATTN_W, s), BF16),
            jax.ShapeDtypeStruct((b, ATTN_W, s), BF16),
            jax.ShapeDtypeStruct((b, N_HEADS // 2, 2, s), F32),
            jax.ShapeDtypeStruct((b, s, POOL_W), BF16),
            *cast_shape,
        ],
        scratch_shapes=[
            pltpu.VMEM((N_HEADS, LANES), F32),
            pltpu.VMEM((HALO, POOL_W), F32),
            pltpu.VMEM((ATTN_W + POOL_W, d), BF16),
            pltpu.VMEM((2 * ATTN_W + WT_F_ROWS, d), BF16),
            pltpu.VMEM((tm, tm), BF16),
        ],
        compiler_params=pltpu.CompilerParams(
            dimension_semantics=("arbitrary", "arbitrary"), vmem_limit_bytes=VMEM_LIMIT),
        name="premix",
    )(x, mod, g1, w_in_t, b_f, pool_w, pool_scale, w_out, ffn_up, ffn_down)


def _attn_kernel(k_ref, fk_ref, qt_ref, vt_ref, ft_ref, o_ref, s_scr, p_scr, *, tq, tk, seq, pairs):
    assert tq % tk == 0
    first_pair = pl.program_id(1) * pairs
    row = lax.broadcasted_iota(jnp.int32, (LANES, tq), 0)
    key_pos = lax.broadcasted_iota(jnp.int32, (tk, tq), 0)
    qry_pos = lax.broadcasted_iota(jnp.int32, (tk, tq), 1)
    ones_rows = jnp.ones((V_ONES_ROWS, tk), BF16)

    def scores(pr, t, q_all):
        r0 = t * tk
        k_aug = jnp.concatenate([k_ref[r0:r0 + tk, pr * LANES:(pr + 1) * LANES],
                                 fk_ref[r0:r0 + tk, :]], axis=1)
        s_scr[pr, t % 2] = jnp.dot(k_aug, q_all, preferred_element_type=F32)

    def softmax(pr, t, ms, mask):
        new_ms, alphas = [], []
        for hh in range(2):
            s = s_scr[pr, t % 2, :, hh * tq:(hh + 1) * tq]
            if mask is not None:
                s = jnp.where(mask, s, -jnp.inf)
            m_new = jnp.maximum(ms[hh], jnp.max(s, axis=0, keepdims=True))
            alphas.append(jnp.exp2(ms[hh] - m_new))
            new_ms.append(m_new)
            p_scr[pr, t % 2, :, hh * tq:(hh + 1) * tq] = jnp.exp2(s - m_new).astype(BF16)
        return new_ms, alphas

    def accumulate(pr, t, alphas, accs):
        r0 = t * tk
        out = []
        for hh in range(2):
            v0 = pr * LANES + hh * HEAD_DIM
            v_aug = jnp.concatenate([vt_ref[v0:v0 + HEAD_DIM, r0:r0 + tk], ones_rows], axis=0)
            pv = jnp.dot(v_aug, p_scr[pr, t % 2, :, hh * tq:(hh + 1) * tq],
                         preferred_element_type=F32)
            out.append(alphas[hh] * accs[hh] + pv)
        return out

    def build_q(pr, c0):
        qt2 = qt_ref[pr * LANES:(pr + 1) * LANES, c0:c0 + tq]
        q_augs = []
        for hh in range(2):
            head = 2 * (first_pair + pr) + hh
            q_h = jnp.where((row >= hh * HEAD_DIM) & (row < (hh + 1) * HEAD_DIM), qt2,
                            jnp.zeros_like(qt2))
            pieces = _split3(ft_ref[pr, hh:hh + 1, c0:c0 + tq])
            fblk = jnp.where((row < F_PIECES * N_HEADS) & (row % N_HEADS == head), 1.0, 0.0)
            for j in range(F_PIECES):
                fblk = jnp.where(row == ONES_LANE + j, pieces[j], fblk)
            q_augs.append(jnp.concatenate([q_h, fblk.astype(BF16)], axis=0))
        return jnp.concatenate(q_augs, axis=1)

    prs = range(pairs)
    for qb in range(seq // tq):
        c0 = qb * tq
        q_all = [build_q(pr, c0) for pr in prs]
        n_full = (tq // tk) * qb
        n_tiles = n_full + tq // tk
        ms = [[jnp.full((1, tq), NEG_BIG, F32) for _ in range(2)] for _ in prs]
        accs = [[jnp.zeros((HEAD_DIM + V_ONES_ROWS, tq), F32) for _ in range(2)] for _ in prs]
        pending = None
        for pr in prs:
            scores(pr, 0, q_all[pr])
        for t in range(n_tiles):
            if t + 1 < n_tiles:
                for pr in prs:
                    scores(pr, t + 1, q_all[pr])
            mask = None if t < n_full else (key_pos + (t - n_full) * tk <= qry_pos)
            alphas = [None] * pairs
            for pr in prs:
                ms[pr], alphas[pr] = softmax(pr, t, ms[pr], mask)
            if pending is not None:
                for pr in prs:
                    accs[pr] = accumulate(pr, pending[0], pending[1][pr], accs[pr])
            pending = (t, alphas)
        for pr in prs:
            accs[pr] = accumulate(pr, pending[0], pending[1][pr], accs[pr])
            out_t = jnp.concatenate(
                [a[:HEAD_DIM] / a[HEAD_DIM:HEAD_DIM + 1] for a in accs[pr]], axis=0)
            o_ref[c0:c0 + tq, pr * LANES:(pr + 1) * LANES] = out_t.T.astype(BF16)


def _attention(k, fk, qt, vt, ft):
    b, s, _ = k.shape
    pp = ATTN_PAIRS
    w = pp * LANES
    kern = functools.partial(_attn_kernel, tq=TQ_ATTN, tk=TK_ATTN, seq=s, pairs=pp)
    return pl.pallas_call(
        kern,
        grid=(b, N_HEADS // 2 // pp),
        in_specs=[
            pl.BlockSpec((None, s, w), lambda bi, p: (bi, 0, p)),
            pl.BlockSpec((None, s, LANES), lambda bi, p: (bi, 0, 0)),
            pl.BlockSpec((None, w, s), lambda bi, p: (bi, p, 0)),
            pl.BlockSpec((None, w, s), lambda bi, p: (bi, p, 0)),
            pl.BlockSpec((None, pp, 2, s), lambda bi, p: (bi, p, 0, 0)),
        ],
        out_specs=pl.BlockSpec((None, s, w), lambda bi, p: (bi, 0, p)),
        out_shape=jax.ShapeDtypeStruct((b, s, ATTN_W), BF16),
        scratch_shapes=[
            pltpu.VMEM((pp, 2, TK_ATTN, 2 * TQ_ATTN), F32),
            pltpu.VMEM((pp, 2, TK_ATTN, 2 * TQ_ATTN), BF16),
        ],
        compiler_params=pltpu.CompilerParams(
            dimension_semantics=("arbitrary", "arbitrary"), vmem_limit_bytes=VMEM_LIMIT),
        name="fox_attention",
    )(k, fk, qt, vt, ft)


def _ffn_kernel(x_ref, attn_ref, pool_ref, mod_ref, g_ref, wo_ref, wup_ref, cw_ref, cb_ref,
                wdn_ref, fg_ref, o_ref, h_ref, act_ref, carry_ref, *, tm, layer, final):
    i = pl.program_id(1)

    @pl.when(i == 0)
    def _():
        carry_ref[...] = jnp.zeros_like(carry_ref)

    mixed = jnp.concatenate([attn_ref[...], pool_ref[...]], axis=1)
    y = jnp.dot(mixed, wo_ref[...], preferred_element_type=F32)
    seq = pl.ds(pl.program_id(0), 1)
    x1 = x_ref[...] + mod_ref[2, seq, :] * y
    o_ref[...] = x1
    ms = jnp.mean(x1 * x1, axis=-1, keepdims=True)
    scale = g_ref[layer:layer + 1, :] * (1.0 + mod_ref[4, seq, :])
    h_ref[...] = ((x1 * lax.rsqrt(ms + EPS)) * scale + mod_ref[3, seq, :]).astype(BF16)

    def conv_branch(c0):
        cols = slice(c0, c0 + FF_CHUNK)
        a = jnp.dot(h_ref[...], wup_ref[:, cols], preferred_element_type=F32)
        ext = jnp.concatenate([carry_ref[:, cols], a], axis=0)
        carry_ref[:, cols] = a[tm - SUBLANES:tm, :]
        a1 = pltpu.roll(ext, 1, axis=0)[SUBLANES:]
        a2 = pltpu.roll(ext, 2, axis=0)[SUBLANES:]
        return (cb_ref[layer:layer + 1, cols] + a2 * cw_ref[layer, 0:1, cols]
                + a1 * cw_ref[layer, 1:2, cols] + a * cw_ref[layer, 2:3, cols])

    for c in range(N_FF_CHUNKS):
        gate = conv_branch(c * FF_CHUNK)
        val = conv_branch(D_FF + c * FF_CHUNK)
        half_gv = (0.5 * gate) * val
        act_ref[:, c * FF_CHUNK:(c + 1) * FF_CHUNK] = (
            half_gv + half_gv * jnp.tanh(0.5 * gate)).astype(BF16)

    y2 = jnp.dot(act_ref[...], wdn_ref[...], preferred_element_type=F32)
    x2 = o_ref[...] + mod_ref[5, seq, :] * y2
    if final:
        ms2 = jnp.mean(x2 * x2, axis=-1, keepdims=True)
        x2 = (x2 * lax.rsqrt(ms2 + EPS)) * fg_ref[...]
    o_ref[...] = x2


def _ffn(x, attn, pool, mod, g2, wo, wup, cw, cb, wdn, fg, layer, final):
    b, s, d = x.shape
    tm = TM_FFN
    kern = functools.partial(_ffn_kernel, tm=tm, layer=layer, final=final)
    row = lambda bi, i: (bi, i, 0)
    const2 = lambda bi, i: (0, 0)
    once = pl.Buffered(1)
    return pl.pallas_call(
        kern,
        grid=(b, s // tm),
        in_specs=[
            pl.BlockSpec((None, tm, d), row),
            pl.BlockSpec((None, tm, ATTN_W), row),
            pl.BlockSpec((None, tm, POOL_W), row),
            pl.BlockSpec((N_MOD, b, d), lambda bi, i: (0, 0, 0)),
            pl.BlockSpec(g2.shape, const2),
            pl.BlockSpec((d, d), const2, pipeline_mode=once),
            pl.BlockSpec((d, 2 * D_FF), const2, pipeline_mode=once),
            pl.BlockSpec(cw.shape, lambda bi, i: (0, 0, 0)),
            pl.BlockSpec(cb.shape, const2),
            pl.BlockSpec((D_FF, d), const2, pipeline_mode=once),
            pl.BlockSpec((1, d), const2),
        ],
        out_specs=pl.BlockSpec((None, tm, d), row),
        out_shape=jax.ShapeDtypeStruct((b, s, d), F32),
        scratch_shapes=[
            pltpu.VMEM((tm, d), BF16),
            pltpu.VMEM((tm, D_FF), BF16),
            pltpu.VMEM((SUBLANES, 2 * D_FF), F32),
        ],
        compiler_params=pltpu.CompilerParams(
            dimension_semantics=("arbitrary", "arbitrary"), vmem_limit_bytes=VMEM_LIMIT),
        name="outproj_convffn",
    )(x, attn, pool, mod, g2, wo, wup, cw, cb, wdn, fg)


def kernel(x, c, mod_w, mod_b, norm1_g, norm2_g, w_in, b_f, pool_w, pool_scale, w_out, ffn_up,
           ffn_conv_w, ffn_conv_b, ffn_down, final_g):
    b, s, d = x.shape
    depth = mod_w.shape[0]
    assert (d, s % TM_PREMIX, s % TQ_ATTN, s % TM_FFN) == (D_MODEL, 0, 0, 0)

    mod = _modulation(c, mod_w, mod_b)
    fg = final_g.reshape(1, d)
    bf = b_f.reshape(depth, N_HEADS, 1)
    w_in_t = jnp.swapaxes(w_in, 1, 2)

    for l in range(depth):
        k, fk, qt, vt, ft, pool, wo, wup, wdn = _premix(
            x, mod[l], norm1_g, w_in_t, l, bf, pool_w, pool_scale, w_out, ffn_up, ffn_down)
        attn = _attention(k, fk, qt, vt, ft)
        x = _ffn(x, attn, pool, mod[l], norm2_g, wo, wup, ffn_conv_w, ffn_conv_b, wdn, fg,
                 layer=l, final=(l == depth - 1))
    return x
```

```python
import functools

import jax
import jax.numpy as jnp
from jax import lax
from jax.experimental import pallas as pl
from jax.experimental.pallas import tpu as pltpu

F32 = jnp.float32
BF16 = jnp.bfloat16

D_MODEL = 1024
N_HEADS = 8
HEAD_DIM = 64
ATTN_W = N_HEADS * HEAD_DIM
POOL_WINDOWS = (2, 4, 8, 16)
POOL_C = 128
POOL_W = POOL_C * len(POOL_WINDOWS)
D_FF = 2816
CONV_W = 3
N_MOD = 6
MOD_VECS_PER_STEP = 2
EPS = 1e-6

LANES = 128
SUBLANES = 8
BF16_SUBLANES = 16
HALO = 16
TRI_ROWS = 256
FF_CHUNK = 256
N_FF_CHUNKS = D_FF // FF_CHUNK
F_PIECES = 3
WT_F_ROWS = 16
ONES_LANE = 32
NEG_BIG = -1e30
LOG2E = 1.4426950408889634
V_ONES_ROWS = 16

TM_PREMIX = 1024
TQ_ATTN = 512
TK_ATTN = 256
ATTN_PAIRS = 1
TM_FFN = 1024
V7X_VMEM_BYTES = 64 * 1024 * 1024
VMEM_LIMIT = V7X_VMEM_BYTES - 8 * 1024 * 1024


def _split3(x):
    hi = x.astype(BF16).astype(F32)
    r = x - hi
    mid = r.astype(BF16).astype(F32)
    lo = (r - mid).astype(BF16).astype(F32)
    return hi, mid, lo


def _mod_kernel(c_ref, w_ref, b_ref, o_ref):
    c = c_ref[...]
    ca = (c * jax.nn.sigmoid(c)).astype(BF16)
    res = jnp.dot(ca, w_ref[...].astype(BF16), preferred_element_type=F32) + b_ref[...]
    d = c.shape[1]
    for v in range(o_ref.shape[0]):
        o_ref[v] = res[:, v * d:(v + 1) * d]


def _modulation(c, mod_w, mod_b):
    depth, d, n = mod_w.shape
    b = c.shape[0]
    cols = MOD_VECS_PER_STEP * d
    return pl.pallas_call(
        _mod_kernel,
        grid=(depth, n // cols),
        in_specs=[
            pl.BlockSpec((b, d), lambda l, j: (0, 0)),
            pl.BlockSpec((None, d, cols), lambda l, j: (l, 0, j)),
            pl.BlockSpec((None, 1, cols), lambda l, j: (l, 0, j)),
        ],
        out_specs=pl.BlockSpec((None, MOD_VECS_PER_STEP, b, d), lambda l, j: (l, j, 0, 0)),
        out_shape=jax.ShapeDtypeStruct((depth, n // d, b, d), F32),
        compiler_params=pltpu.CompilerParams(
            dimension_semantics=("arbitrary", "arbitrary"), vmem_limit_bytes=VMEM_LIMIT),
        name="modulation",
    )(c, mod_w, mod_b.reshape(depth, 1, n))


def _premix_kernel(x_ref, mod_ref, g_ref, w_ref, bf_ref, pw_ref, ps_ref,
                   wo_ref, wup_ref, wdn_ref,
                   k_ref, fk_ref, qt_ref, vt_ref, ft_ref, pool_ref, wo_bf_ref, wup_bf_ref, wdn_bf_ref,
                   carry_ref, ubuf_ref, wku_ref, wt_ref, tri_ref, *, tm, layer):
    i = pl.program_id(1)

    @pl.when((pl.program_id(0) == 0) & (i == 0))
    def _():
        a0, a1, a2, a3 = ATTN_W, 2 * ATTN_W, 3 * ATTN_W, 3 * ATTN_W + N_HEADS
        wku_ref[:ATTN_W, :] = w_ref[a0:a1, :].astype(BF16)
        wku_ref[ATTN_W:, :] = w_ref[a3:, :].astype(BF16)
        wt_ref[0:WT_F_ROWS, :] = jnp.concatenate(
            [w_ref[a2:a3, :], jnp.zeros((WT_F_ROWS - N_HEADS, w_ref.shape[1]), F32)], axis=0).astype(BF16)
        wt_ref[WT_F_ROWS:WT_F_ROWS + ATTN_W, :] = (
            w_ref[0:a0, :] * (HEAD_DIM ** -0.5 * LOG2E)).astype(BF16)
        wt_ref[WT_F_ROWS + ATTN_W:, :] = w_ref[a1:a2, :].astype(BF16)
        for r0 in range(0, tm, TRI_ROWS):
            rr = r0 + lax.broadcasted_iota(jnp.int32, (TRI_ROWS, tm), 0)
            cc = lax.broadcasted_iota(jnp.int32, (TRI_ROWS, tm), 1)
            tri_ref[r0:r0 + TRI_ROWS, :] = jnp.where(rr <= cc, 1.0, 0.0).astype(BF16)

    @pl.when(i == 0)
    def _():
        carry_ref[...] = jnp.zeros_like(carry_ref)
        ubuf_ref[...] = jnp.zeros_like(ubuf_ref)

    x = x_ref[...]
    ms = jnp.mean(x * x, axis=-1, keepdims=True)
    seq = pl.ds(pl.program_id(0), 1)
    scale = g_ref[layer:layer + 1, :] * (1.0 + mod_ref[1, seq, :])
    h = (x * lax.rsqrt(ms + EPS)) * scale + mod_ref[0, seq, :]
    hb = h.astype(BF16)

    ku = lax.dot_general(hb, wku_ref[...], (((1,), (1,)), ((), ())),
                         preferred_element_type=F32)
    k_ref[...] = ku[:, :ATTN_W].astype(BF16)
    u = ku[:, ATTN_W:]

    zt = lax.dot_general(wt_ref[...], hb, (((1,), (1,)), ((), ())),
                         preferred_element_type=F32)
    qt_ref[...] = zt[WT_F_ROWS:WT_F_ROWS + ATTN_W].astype(BF16)
    vt_ref[...] = zt[WT_F_ROWS + ATTN_W:].astype(BF16)

    wo_bf_ref[...] = wo_ref[...].astype(BF16)
    wup_bf_ref[...] = wup_ref[...].astype(BF16)
    wdn_bf_ref[...] = wdn_ref[...].astype(BF16)

    fl = zt[0:N_HEADS] + bf_ref[...]
    logf = jnp.minimum(fl, 0.0) - jnp.log1p(jnp.exp(-jnp.abs(fl)))
    pieces = jnp.concatenate(list(_split3(logf)) + [jnp.zeros((SUBLANES, tm), F32)], axis=0)
    cs = jnp.dot(pieces.astype(BF16), tri_ref[...], preferred_element_type=F32)
    f_cum = (cs[0:8] + cs[8:16] + cs[16:24]) + carry_ref[:, 0:1]
    carry_ref[...] = jnp.broadcast_to(f_cum[:, tm - 1:tm], carry_ref.shape)
    f_log2 = f_cum * LOG2E
    for pair in range(N_HEADS // 2):
        ft_ref[pair] = f_log2[2 * pair:2 * pair + 2]

    blk = jnp.concatenate(list(_split3(-f_log2))
                          + [jnp.zeros((LANES - F_PIECES * N_HEADS, tm), F32)], axis=0)
    blk_t = blk.T
    lane = lax.broadcasted_iota(jnp.int32, (tm, LANES), 1)
    blk_t = jnp.where((lane >= ONES_LANE) & (lane < ONES_LANE + F_PIECES), 1.0, blk_t)
    fk_ref[...] = blk_t.astype(BF16)

    pos = i * tm + lax.broadcasted_iota(jnp.int32, (tm, POOL_C), 0)
    for g, w in enumerate(POOL_WINDOWS):
        c0 = g * POOL_C
        ug = u[:, c0:c0 + POOL_C]
        ws = jnp.concatenate([ubuf_ref[:, c0:c0 + POOL_C], ug], axis=0)
        shift = 1
        while shift < w:
            ws = ws + pltpu.roll(ws, shift, axis=0)
            shift *= 2
        cnt = jnp.minimum(pos + 1, w).astype(F32)
        d = ws[HALO:] / cnt - ug
        pg = jnp.dot(d.astype(BF16), pw_ref[g].astype(BF16), preferred_element_type=F32)
        pool_ref[:, c0:c0 + POOL_C] = (pg * ps_ref[layer:layer + 1, c0:c0 + POOL_C]).astype(BF16)
    ubuf_ref[...] = u[tm - HALO:, :]


def _premix(x, mod, g1, w_in_t, layer, b_f, pool_w, pool_scale, w_out, ffn_up, ffn_down):
    b, s, d = x.shape
    tm = TM_PREMIX
    once = pl.Buffered(1)
    n_i = s // tm
    n_steps = b * n_i

    def cast_specs(w):
        rows, cols = w.shape[1:]
        share = 1
        while rows % (n_steps // share) or (rows // (n_steps // share)) % BF16_SUBLANES:
            share *= 2
        slab = rows // (n_steps // share)
        return (pl.BlockSpec((None, slab, cols), lambda bi, i: (layer, (bi * n_i + i) // share, 0)),
                pl.BlockSpec((slab, cols), lambda bi, i: ((bi * n_i + i) // share, 0)),
                jax.ShapeDtypeStruct((rows, cols), BF16))

    cast_in, cast_out, cast_shape = zip(*(cast_specs(w) for w in (w_out, ffn_up, ffn_down)))
    kern = functools.partial(_premix_kernel, tm=tm, layer=layer)
    row = lambda bi, i: (bi, i, 0)
    col = lambda bi, i: (bi, 0, i)
    const2 = lambda bi, i: (0, 0)
    per_layer = lambda bi, i: (layer, 0, 0)
    return pl.pallas_call(
        kern,
        grid=(b, s // tm),
        in_specs=[
            pl.BlockSpec((None, tm, d), row),
            pl.BlockSpec((None, N_MOD, b, d), lambda bi, i: (layer, 0, 0, 0)),
            pl.BlockSpec(g1.shape, const2),
            pl.BlockSpec((None, w_in_t.shape[1], d), per_layer, pipeline_mode=once),
            pl.BlockSpec((None, N_HEADS, 1), per_layer),
            pl.BlockSpec((None,) + pool_w.shape[1:], lambda bi, i: (layer, 0, 0, 0)),
            pl.BlockSpec(pool_scale.shape, const2),
            *cast_in,
        ],
        out_specs=[
            pl.BlockSpec((None, tm, ATTN_W), row),
            pl.BlockSpec((None, tm, LANES), row),
            pl.BlockSpec((None, ATTN_W, tm), col),
            pl.BlockSpec((None, ATTN_W, tm), col),
            pl.BlockSpec((None, N_HEADS // 2, 2, tm), lambda bi, i: (bi, 0, 0, i)),
            pl.BlockSpec((None, tm, POOL_W), row),
            *cast_out,
        ],
        out_shape=[
            jax.ShapeDtypeStruct((b, s, ATTN_W), BF16),
            jax.ShapeDtypeStruct((b, s, LANES), BF16),
            jax.ShapeDtypeStruct((b, ATTN_W, s), BF16),
            jax.ShapeDtypeStruct((b, ATTN_W, s), BF16),
            jax.ShapeDtypeStruct((b, N_HEADS // 2, 2, s), F32),
            jax.ShapeDtypeStruct((b, s, POOL_W), BF16),
            *cast_shape,
        ],
        scratch_shapes=[
            pltpu.VMEM((N_HEADS, LANES), F32),
            pltpu.VMEM((HALO, POOL_W), F32),
            pltpu.VMEM((ATTN_W + POOL_W, d), BF16),
            pltpu.VMEM((2 * ATTN_W + WT_F_ROWS, d), BF16),
            pltpu.VMEM((tm, tm), BF16),
        ],
        compiler_params=pltpu.CompilerParams(
            dimension_semantics=("arbitrary", "arbitrary"), vmem_limit_bytes=VMEM_LIMIT),
        name="premix",
    )(x, mod, g1, w_in_t, b_f, pool_w, pool_scale, w_out, ffn_up, ffn_down)


def _attn_kernel(k_ref, fk_ref, qt_ref, vt_ref, ft_ref, o_ref, s_scr, p_scr, *, tq, tk, seq, pairs):
    assert tq % tk == 0
    first_pair = pl.program_id(1) * pairs
    row = lax.broadcasted_iota(jnp.int32, (LANES, tq), 0)
    key_pos = lax.broadcasted_iota(jnp.int32, (tk, tq), 0)
    qry_pos = lax.broadcasted_iota(jnp.int32, (tk, tq), 1)
    ones_rows = jnp.ones((V_ONES_ROWS, tk), BF16)

    def scores(pr, t, q_all):
        r0 = t * tk
        k_aug = jnp.concatenate([k_ref[r0:r0 + tk, pr * LANES:(pr + 1) * LANES],
                                 fk_ref[r0:r0 + tk, :]], axis=1)
        s_scr[pr, t % 2] = jnp.dot(k_aug, q_all, preferred_element_type=F32)

    def softmax(pr, t, ms, mask):
        new_ms, alphas = [], []
        for hh in range(2):
            s = s_scr[pr, t % 2, :, hh * tq:(hh + 1) * tq]
            if mask is not None:
                s = jnp.where(mask, s, -jnp.inf)
            m_new = jnp.maximum(ms[hh], jnp.max(s, axis=0, keepdims=True))
            alphas.append(jnp.exp2(ms[hh] - m_new))
            new_ms.append(m_new)
            p_scr[pr, t % 2, :, hh * tq:(hh + 1) * tq] = jnp.exp2(s - m_new).astype(BF16)
        return new_ms, alphas

    def accumulate(pr, t, alphas, accs):
        r0 = t * tk
        out = []
        for hh in range(2):
            v0 = pr * LANES + hh * HEAD_DIM
            v_aug = jnp.concatenate([vt_ref[v0:v0 + HEAD_DIM, r0:r0 + tk], ones_rows], axis=0)
            pv = jnp.dot(v_aug, p_scr[pr, t % 2, :, hh * tq:(hh + 1) * tq],
                         preferred_element_type=F32)
            out.append(alphas[hh] * accs[hh] + pv)
        return out

    def build_q(pr, c0):
        qt2 = qt_ref[pr * LANES:(pr + 1) * LANES, c0:c0 + tq]
        q_augs = []
        for hh in range(2):
            head = 2 * (first_pair + pr) + hh
            q_h = jnp.where((row >= hh * HEAD_DIM) & (row < (hh + 1) * HEAD_DIM), qt2,
                            jnp.zeros_like(qt2))
            pieces = _split3(ft_ref[pr, hh:hh + 1, c0:c0 + tq])
            fblk = jnp.where((row < F_PIECES * N_HEADS) & (row % N_HEADS == head), 1.0, 0.0)
            for j in range(F_PIECES):
                fblk = jnp.where(row == ONES_LANE + j, pieces[j], fblk)
            q_augs.append(jnp.concatenate([q_h, fblk.astype(BF16)], axis=0))
        return jnp.concatenate(q_augs, axis=1)

    prs = range(pairs)
    for qb in range(seq // tq):
        c0 = qb * tq
        q_all = [build_q(pr, c0) for pr in prs]
        n_full = (tq // tk) * qb
        n_tiles = n_full + tq // tk
        ms = [[jnp.full((1, tq), NEG_BIG, F32) for _ in range(2)] for _ in prs]
        accs = [[jnp.zeros((HEAD_DIM + V_ONES_ROWS, tq), F32) for _ in range(2)] for _ in prs]
        pending = None
        for pr in prs:
            scores(pr, 0, q_all[pr])
        for t in range(n_tiles):
            if t + 1 < n_tiles:
                for pr in prs:
                    scores(pr, t + 1, q_all[pr])
            mask = None if t < n_full else (key_pos + (t - n_full) * tk <= qry_pos)
            alphas = [None] * pairs
            for pr in prs:
                ms[pr], alphas[pr] = softmax(pr, t, ms[pr], mask)
            if pending is not None:
                for pr in prs:
                    accs[pr] = accumulate(pr, pending[0], pending[1][pr], accs[pr])
            pending = (t, alphas)
        for pr in prs:
            accs[pr] = accumulate(pr, pending[0], pending[1][pr], accs[pr])
            out_t = jnp.concatenate(
                [a[:HEAD_DIM] / a[HEAD_DIM:HEAD_DIM + 1] for a in accs[pr]], axis=0)
            o_ref[c0:c0 + tq, pr * LANES:(pr + 1) * LANES] = out_t.T.astype(BF16)


def _attention(k, fk, qt, vt, ft):
    b, s, _ = k.shape
    pp = ATTN_PAIRS
    w = pp * LANES
    kern = functools.partial(_attn_kernel, tq=TQ_ATTN, tk=TK_ATTN, seq=s, pairs=pp)
    return pl.pallas_call(
        kern,
        grid=(b, N_HEADS // 2 // pp),
        in_specs=[
            pl.BlockSpec((None, s, w), lambda bi, p: (bi, 0, p)),
            pl.BlockSpec((None, s, LANES), lambda bi, p: (bi, 0, 0)),
            pl.BlockSpec((None, w, s), lambda bi, p: (bi, p, 0)),
            pl.BlockSpec((None, w, s), lambda bi, p: (bi, p, 0)),
            pl.BlockSpec((None, pp, 2, s), lambda bi, p: (bi, p, 0, 0)),
        ],
        out_specs=pl.BlockSpec((None, s, w), lambda bi, p: (bi, 0, p)),
        out_shape=jax.ShapeDtypeStruct((b, s, ATTN_W), BF16),
        scratch_shapes=[
            pltpu.VMEM((pp, 2, TK_ATTN, 2 * TQ_ATTN), F32),
            pltpu.VMEM((pp, 2, TK_ATTN, 2 * TQ_ATTN), BF16),
        ],
        compiler_params=pltpu.CompilerParams(
            dimension_semantics=("arbitrary", "arbitrary"), vmem_limit_bytes=VMEM_LIMIT),
        name="fox_attention",
    )(k, fk, qt, vt, ft)


def _ffn_kernel(x_ref, attn_ref, pool_ref, mod_ref, g_ref, wo_ref, wup_ref, cw_ref, cb_ref,
                wdn_ref, fg_ref, o_ref, h_ref, act_ref, carry_ref, *, tm, layer, final):
    i = pl.program_id(1)

    @pl.when(i == 0)
    def _():
        carry_ref[...] = jnp.zeros_like(carry_ref)

    mixed = jnp.concatenate([attn_ref[...], pool_ref[...]], axis=1)
    y = jnp.dot(mixed, wo_ref[...], preferred_element_type=F32)
    seq = pl.ds(pl.program_id(0), 1)
    x1 = x_ref[...] + mod_ref[2, seq, :] * y
    o_ref[...] = x1
    ms = jnp.mean(x1 * x1, axis=-1, keepdims=True)
    scale = g_ref[layer:layer + 1, :] * (1.0 + mod_ref[4, seq, :])
    h_ref[...] = ((x1 * lax.rsqrt(ms + EPS)) * scale + mod_ref[3, seq, :]).astype(BF16)

    def conv_branch(c0):
        cols = slice(c0, c0 + FF_CHUNK)
        a = jnp.dot(h_ref[...], wup_ref[:, cols], preferred_element_type=F32)
        ext = jnp.concatenate([carry_ref[:, cols], a], axis=0)
        carry_ref[:, cols] = a[tm - SUBLANES:tm, :]
        a1 = pltpu.roll(ext, 1, axis=0)[SUBLANES:]
        a2 = pltpu.roll(ext, 2, axis=0)[SUBLANES:]
        return (cb_ref[layer:layer + 1, cols] + a2 * cw_ref[layer, 0:1, cols]
                + a1 * cw_ref[layer, 1:2, cols] + a * cw_ref[layer, 2:3, cols])

    for c in range(N_FF_CHUNKS):
        gate = conv_branch(c * FF_CHUNK)
        val = conv_branch(D_FF + c * FF_CHUNK)
        half_gv = (0.5 * gate) * val
        act_ref[:, c * FF_CHUNK:(c + 1) * FF_CHUNK] = (
            half_gv + half_gv * jnp.tanh(0.5 * gate)).astype(BF16)

    y2 = jnp.dot(act_ref[...], wdn_ref[...], preferred_element_type=F32)
    x2 = o_ref[...] + mod_ref[5, seq, :] * y2
    if final:
        ms2 = jnp.mean(x2 * x2, axis=-1, keepdims=True)
        x2 = (x2 * lax.rsqrt(ms2 + EPS)) * fg_ref[...]
    o_ref[...] = x2


def _ffn(x, attn, pool, mod, g2, wo, wup, cw, cb, wdn, fg, layer, final):
    b, s, d = x.shape
    tm = TM_FFN
    kern = functools.partial(_ffn_kernel, tm=tm, layer=layer, final=final)
    row = lambda bi, i: (bi, i, 0)
    const2 = lambda bi, i: (0, 0)
    once = pl.Buffered(1)
    return pl.pallas_call(
        kern,
        grid=(b, s // tm),
        in_specs=[
            pl.BlockSpec((None, tm, d), row),
            pl.BlockSpec((None, tm, ATTN_W), row),
            pl.BlockSpec((None, tm, POOL_W), row),
            pl.BlockSpec((None, N_MOD, b, d), lambda bi, i: (layer, 0, 0, 0)),
            pl.BlockSpec(g2.shape, const2),
            pl.BlockSpec((d, d), const2, pipeline_mode=once),
            pl.BlockSpec((d, 2 * D_FF), const2, pipeline_mode=once),
            pl.BlockSpec(cw.shape, lambda bi, i: (0, 0, 0)),
            pl.BlockSpec(cb.shape, const2),
            pl.BlockSpec((D_FF, d), const2, pipeline_mode=once),
            pl.BlockSpec((1, d), const2),
        ],
        out_specs=pl.BlockSpec((None, tm, d), row),
        out_shape=jax.ShapeDtypeStruct((b, s, d), F32),
        scratch_shapes=[
            pltpu.VMEM((tm, d), BF16),
            pltpu.VMEM((tm, D_FF), BF16),
            pltpu.VMEM((SUBLANES, 2 * D_FF), F32),
        ],
        compiler_params=pltpu.CompilerParams(
            dimension_semantics=("arbitrary", "arbitrary"), vmem_limit_bytes=VMEM_LIMIT),
        name="outproj_convffn",
    )(x, attn, pool, mod, g2, wo, wup, cw, cb, wdn, fg)


def kernel(x, c, mod_w, mod_b, norm1_g, norm2_g, w_in, b_f, pool_w, pool_scale, w_out, ffn_up,
           ffn_conv_w, ffn_conv_b, ffn_down, final_g):
    b, s, d = x.shape
    depth = mod_w.shape[0]
    assert (d, s % TM_PREMIX, s % TQ_ATTN, s % TM_FFN) == (D_MODEL, 0, 0, 0)

    mod = _modulation(c, mod_w, mod_b)
    fg = final_g.reshape(1, d)
    bf = b_f.reshape(depth, N_HEADS, 1)
    w_in_t = jnp.swapaxes(w_in, 1, 2)

    for l in range(depth):
        k, fk, qt, vt, ft, pool, wo, wup, wdn = _premix(
            x, mod, norm1_g, w_in_t, l, bf, pool_w, pool_scale, w_out, ffn_up, ffn_down)
        attn = _attention(k, fk, qt, vt, ft)
        x = _ffn(x, attn, pool, mod, norm2_g, wo, wup, ffn_conv_w, ffn_conv_b, wdn, fg,
                 layer=l, final=(l == depth - 1))
    return x
```

```python
import functools

import jax
import jax.numpy as jnp
from jax import lax
from jax.experimental import pallas as pl
from jax.experimental.pallas import tpu as pltpu

F32 = jnp.float32
BF16 = jnp.bfloat16

D_MODEL = 1024
N_HEADS = 8
HEAD_DIM = 64
ATTN_W = N_HEADS * HEAD_DIM
POOL_WINDOWS = (2, 4, 8, 16)
POOL_C = 128
POOL_W = POOL_C * len(POOL_WINDOWS)
D_FF = 2816
CONV_W = 3
N_MOD = 6
MOD_VECS_PER_STEP = 2
EPS = 1e-6

LANES = 128
SUBLANES = 8
BF16_SUBLANES = 16
HALO = 16
TRI = 256
FF_CHUNK = 256
N_FF_CHUNKS = D_FF // FF_CHUNK
F_PIECES = 3
WT_F_ROWS = 16
ONES_LANE = 32
NEG_BIG = -1e30
LOG2E = 1.4426950408889634
V_ONES_ROWS = 16

TM_PREMIX = 1024
TQ_ATTN = 512
TK_ATTN = 256
ATTN_PAIRS = 1
TM_FFN = 1024
V7X_VMEM_BYTES = 64 * 1024 * 1024
VMEM_LIMIT = V7X_VMEM_BYTES - 8 * 1024 * 1024


def _split3(x):
    hi = x.astype(BF16).astype(F32)
    r = x - hi
    mid = r.astype(BF16).astype(F32)
    lo = (r - mid).astype(BF16).astype(F32)
    return hi, mid, lo


def _mod_kernel(c_ref, w_ref, b_ref, o_ref):
    c = c_ref[...]
    ca = (c * jax.nn.sigmoid(c)).astype(BF16)
    res = jnp.dot(ca, w_ref[...].astype(BF16), preferred_element_type=F32) + b_ref[...]
    d = c.shape[1]
    for v in range(o_ref.shape[0]):
        o_ref[v] = res[:, v * d:(v + 1) * d]


def _modulation(c, mod_w, mod_b):
    depth, d, n = mod_w.shape
    b = c.shape[0]
    cols = MOD_VECS_PER_STEP * d
    return pl.pallas_call(
        _mod_kernel,
        grid=(depth, n // cols),
        in_specs=[
            pl.BlockSpec((b, d), lambda l, j: (0, 0)),
            pl.BlockSpec((None, d, cols), lambda l, j: (l, 0, j)),
            pl.BlockSpec((None, 1, cols), lambda l, j: (l, 0, j)),
        ],
        out_specs=pl.BlockSpec((None, MOD_VECS_PER_STEP, b, d), lambda l, j: (l, j, 0, 0)),
        out_shape=jax.ShapeDtypeStruct((depth, n // d, b, d), F32),
        compiler_params=pltpu.CompilerParams(
            dimension_semantics=("arbitrary", "arbitrary"), vmem_limit_bytes=VMEM_LIMIT),
        name="modulation",
    )(c, mod_w, mod_b.reshape(depth, 1, n))


def _premix_kernel(x_ref, mod_ref, g_ref, w_ref, bf_ref, pw_ref, ps_ref,
                   wo_ref, wup_ref, wdn_ref,
                   k_ref, fk_ref, qt_ref, vt_ref, ft_ref, pool_ref, wo_bf_ref, wup_bf_ref, wdn_bf_ref,
                   carry_ref, ubuf_ref, wku_ref, wt_ref, tri_ref, *, tm, layer):
    i = pl.program_id(1)

    @pl.when((pl.program_id(0) == 0) & (i == 0))
    def _():
        a0, a1, a2, a3 = ATTN_W, 2 * ATTN_W, 3 * ATTN_W, 3 * ATTN_W + N_HEADS
        wku_ref[:ATTN_W, :] = w_ref[a0:a1, :].astype(BF16)
        wku_ref[ATTN_W:, :] = w_ref[a3:, :].astype(BF16)
        wt_ref[0:WT_F_ROWS, :] = jnp.concatenate(
            [w_ref[a2:a3, :], jnp.zeros((WT_F_ROWS - N_HEADS, w_ref.shape[1]), F32)], axis=0).astype(BF16)
        wt_ref[WT_F_ROWS:WT_F_ROWS + ATTN_W, :] = (
            w_ref[0:a0, :] * (HEAD_DIM ** -0.5 * LOG2E)).astype(BF16)
        wt_ref[WT_F_ROWS + ATTN_W:, :] = w_ref[a1:a2, :].astype(BF16)
        rr = lax.broadcasted_iota(jnp.int32, (TRI, TRI), 0)
        cc = lax.broadcasted_iota(jnp.int32, (TRI, TRI), 1)
        tri_ref[...] = jnp.where(rr <= cc, 1.0, 0.0).astype(BF16)

    @pl.when(i == 0)
    def _():
        carry_ref[...] = jnp.zeros_like(carry_ref)
        ubuf_ref[...] = jnp.zeros_like(ubuf_ref)

    x = x_ref[...]
    ms = jnp.mean(x * x, axis=-1, keepdims=True)
    seq = pl.ds(pl.program_id(0), 1)
    scale = g_ref[layer:layer + 1, :] * (1.0 + mod_ref[1, seq, :])
    h = (x * lax.rsqrt(ms + EPS)) * scale + mod_ref[0, seq, :]
    hb = h.astype(BF16)

    ku = lax.dot_general(hb, wku_ref[...], (((1,), (1,)), ((), ())),
                         preferred_element_type=F32)
    k_ref[...] = ku[:, :ATTN_W].astype(BF16)
    u = ku[:, ATTN_W:]

    zt = lax.dot_general(wt_ref[...], hb, (((1,), (1,)), ((), ())),
                         preferred_element_type=F32)
    qt_ref[...] = zt[WT_F_ROWS:WT_F_ROWS + ATTN_W].astype(BF16)
    vt_ref[...] = zt[WT_F_ROWS + ATTN_W:].astype(BF16)

    wo_bf_ref[...] = wo_ref[...].astype(BF16)
    wup_bf_ref[...] = wup_ref[...].astype(BF16)
    wdn_bf_ref[...] = wdn_ref[...].astype(BF16)

    fl = zt[0:N_HEADS] + bf_ref[...]
    logf = jnp.minimum(fl, 0.0) - jnp.log1p(jnp.exp(-jnp.abs(fl)))
    pieces = jnp.concatenate(list(_split3(logf)) + [jnp.zeros((SUBLANES, tm), F32)],
                             axis=0).astype(BF16)
    blocks, offset = [], carry_ref[:, 0:1]
    for c0 in range(0, tm, TRI):
        cs = jnp.dot(pieces[:, c0:c0 + TRI], tri_ref[...], preferred_element_type=F32)
        f_blk = (cs[0:8] + cs[8:16] + cs[16:24]) + offset
        offset = f_blk[:, TRI - 1:TRI]
        blocks.append(f_blk)
    f_cum = jnp.concatenate(blocks, axis=1)
    carry_ref[...] = jnp.broadcast_to(offset, carry_ref.shape)
    f_log2 = f_cum * LOG2E
    for pair in range(N_HEADS // 2):
        ft_ref[pair] = f_log2[2 * pair:2 * pair + 2]

    blk = jnp.concatenate(list(_split3(-f_log2))
                          + [jnp.zeros((LANES - F_PIECES * N_HEADS, tm), F32)], axis=0)
    blk_t = blk.T
    lane = lax.broadcasted_iota(jnp.int32, (tm, LANES), 1)
    blk_t = jnp.where((lane >= ONES_LANE) & (lane < ONES_LANE + F_PIECES), 1.0, blk_t)
    fk_ref[...] = blk_t.astype(BF16)

    pos = i * tm + lax.broadcasted_iota(jnp.int32, (tm, POOL_C), 0)
    for g, w in enumerate(POOL_WINDOWS):
        c0 = g * POOL_C
        ug = u[:, c0:c0 + POOL_C]
        ws = jnp.concatenate([ubuf_ref[:, c0:c0 + POOL_C], ug], axis=0)
        shift = 1
        while shift < w:
            ws = ws + pltpu.roll(ws, shift, axis=0)
            shift *= 2
        cnt = jnp.minimum(pos + 1, w).astype(F32)
        d = ws[HALO:] / cnt - ug
        pg = jnp.dot(d.astype(BF16), pw_ref[g].astype(BF16), preferred_element_type=F32)
        pool_ref[:, c0:c0 + POOL_C] = (pg * ps_ref[layer:layer + 1, c0:c0 + POOL_C]).astype(BF16)
    ubuf_ref[...] = u[tm - HALO:, :]


def _premix(x, mod, g1, w_in_t, layer, b_f, pool_w, pool_scale, w_out, ffn_up, ffn_down):
    b, s, d = x.shape
    tm = TM_PREMIX
    once = pl.Buffered(1)
    n_i = s // tm
    n_steps = b * n_i

    def cast_specs(w):
        rows, cols = w.shape[1:]
        share = 1
        while rows % (n_steps // share) or (rows // (n_steps // share)) % BF16_SUBLANES:
            share *= 2
        slab = rows // (n_steps // share)
        return (pl.BlockSpec((None, slab, cols), lambda bi, i: (layer, (bi * n_i + i) // share, 0)),
                pl.BlockSpec((slab, cols), lambda bi, i: ((bi * n_i + i) // share, 0)),
                jax.ShapeDtypeStruct((rows, cols), BF16))

    cast_in, cast_out, cast_shape = zip(*(cast_specs(w) for w in (w_out, ffn_up, ffn_down)))
    kern = functools.partial(_premix_kernel, tm=tm, layer=layer)
    row = lambda bi, i: (bi, i, 0)
    col = lambda bi, i: (bi, 0, i)
    const2 = lambda bi, i: (0, 0)
    per_layer = lambda bi, i: (layer, 0, 0)
    return pl.pallas_call(
        kern,
        grid=(b, s // tm),
        in_specs=[
            pl.BlockSpec((None, tm, d), row),
            pl.BlockSpec((None, N_MOD, b, d), lambda bi, i: (layer, 0, 0, 0)),
            pl.BlockSpec(g1.shape, const2),
            pl.BlockSpec((None, w_in_t.shape[1], d), per_layer, pipeline_mode=once),
            pl.BlockSpec((None, N_HEADS, 1), per_layer),
            pl.BlockSpec((None,) + pool_w.shape[1:], lambda bi, i: (layer, 0, 0, 0)),
            pl.BlockSpec(pool_scale.shape, const2),
            *cast_in,
        ],
        out_specs=[
            pl.BlockSpec((None, tm, ATTN_W), row),
            pl.BlockSpec((None, tm, LANES), row),
            pl.BlockSpec((None, ATTN_W, tm), col),
            pl.BlockSpec((None, ATTN_W, tm), col),
            pl.BlockSpec((None, N_HEADS // 2, 2, tm), lambda bi, i: (bi, 0, 0, i)),
            pl.BlockSpec((None, tm, POOL_W), row),
            *cast_out,
        ],
        out_shape=[
            jax.ShapeDtypeStruct((b, s, ATTN_W), BF16),
            jax.ShapeDtypeStruct((b, s, LANES), BF16),
            jax.ShapeDtypeStruct((b, ATTN_W, s), BF16),
            jax.ShapeDtypeStruct((b, ATTN_W, s), BF16),
            jax.ShapeDtypeStruct((b, N_HEADS // 2, 2, s), F32),
            jax.ShapeDtypeStruct((b, s, POOL_W), BF16),
            *cast_shape,
        ],
        scratch_shapes=[
            pltpu.VMEM((N_HEADS, LANES), F32),
            pltpu.VMEM((HALO, POOL_W), F32),
            pltpu.VMEM((ATTN_W + POOL_W, d), BF16),
            pltpu.VMEM((2 * ATTN_W + WT_F_ROWS, d), BF16),
            pltpu.VMEM((TRI, TRI), BF16),
        ],
        compiler_params=pltpu.CompilerParams(
            dimension_semantics=("arbitrary", "arbitrary"), vmem_limit_bytes=VMEM_LIMIT),
        name="premix",
    )(x, mod, g1, w_in_t, b_f, pool_w, pool_scale, w_out, ffn_up, ffn_down)


def _attn_kernel(k_ref, fk_ref, qt_ref, vt_ref, ft_ref, o_ref, s_scr, p_scr, *, tq, tk, seq, pairs):
    assert tq % tk == 0
    first_pair = pl.program_id(1) * pairs
    row = lax.broadcasted_iota(jnp.int32, (LANES, tq), 0)
    key_pos = lax.broadcasted_iota(jnp.int32, (tk, tq), 0)
    qry_pos = lax.broadcasted_iota(jnp.int32, (tk, tq), 1)
    ones_rows = jnp.ones((V_ONES_ROWS, tk), BF16)

    def scores(pr, t, q_all):
        r0 = t * tk
        k_aug = jnp.concatenate([k_ref[r0:r0 + tk, pr * LANES:(pr + 1) * LANES],
                                 fk_ref[r0:r0 + tk, :]], axis=1)
        s_scr[pr, t % 2] = jnp.dot(k_aug, q_all, preferred_element_type=F32)

    def softmax(pr, t, ms, mask):
        new_ms, alphas = [], []
        for hh in range(2):
            s = s_scr[pr, t % 2, :, hh * tq:(hh + 1) * tq]
            if mask is not None:
                s = jnp.where(mask, s, -jnp.inf)
            m_new = jnp.maximum(ms[hh], jnp.max(s, axis=0, keepdims=True))
            alphas.append(jnp.exp2(ms[hh] - m_new))
            new_ms.append(m_new)
            p_scr[pr, t % 2, :, hh * tq:(hh + 1) * tq] = jnp.exp2(s - m_new).astype(BF16)
        return new_ms, alphas

    def accumulate(pr, t, alphas, accs):
        r0 = t * tk
        out = []
        for hh in range(2):
            v0 = pr * LANES + hh * HEAD_DIM
            v_aug = jnp.concatenate([vt_ref[v0:v0 + HEAD_DIM, r0:r0 + tk], ones_rows], axis=0)
            pv = jnp.dot(v_aug, p_scr[pr, t % 2, :, hh * tq:(hh + 1) * tq],
                         preferred_element_type=F32)
            out.append(alphas[hh] * accs[hh] + pv)
        return out

    def build_q(pr, c0):
        qt2 = qt_ref[pr * LANES:(pr + 1) * LANES, c0:c0 + tq]
        q_augs = []
        for hh in range(2):
            head = 2 * (first_pair + pr) + hh
            q_h = jnp.where((row >= hh * HEAD_DIM) & (row < (hh + 1) * HEAD_DIM), qt2,
                            jnp.zeros_like(qt2))
            pieces = _split3(ft_ref[pr, hh:hh + 1, c0:c0 + tq])
            fblk = jnp.where((row < F_PIECES * N_HEADS) & (row % N_HEADS == head), 1.0, 0.0)
            for j in range(F_PIECES):
                fblk = jnp.where(row == ONES_LANE + j, pieces[j], fblk)
            q_augs.append(jnp.concatenate([q_h, fblk.astype(BF16)], axis=0))
        return jnp.concatenate(q_augs, axis=1)

    prs = range(pairs)
    for qb in range(seq // tq):
        c0 = qb * tq
        q_all = [build_q(pr, c0) for pr in prs]
        n_full = (tq // tk) * qb
        n_tiles = n_full + tq // tk
        ms = [[jnp.full((1, tq), NEG_BIG, F32) for _ in range(2)] for _ in prs]
        accs = [[jnp.zeros((HEAD_DIM + V_ONES_ROWS, tq), F32) for _ in range(2)] for _ in prs]
        pending = None
        for pr in prs:
            scores(pr, 0, q_all[pr])
        for t in range(n_tiles):
            if t + 1 < n_tiles:
                for pr in prs:
                    scores(pr, t + 1, q_all[pr])
            mask = None if t < n_full else (key_pos + (t - n_full) * tk <= qry_pos)
            alphas = [None] * pairs
            for pr in prs:
                ms[pr], alphas[pr] = softmax(pr, t, ms[pr], mask)
            if pending is not None:
                for pr in prs:
                    accs[pr] = accumulate(pr, pending[0], pending[1][pr], accs[pr])
            pending = (t, alphas)
        for pr in prs:
            accs[pr] = accumulate(pr, pending[0], pending[1][pr], accs[pr])
            out_t = jnp.concatenate(
                [a[:HEAD_DIM] / a[HEAD_DIM:HEAD_DIM + 1] for a in accs[pr]], axis=0)
            o_ref[c0:c0 + tq, pr * LANES:(pr + 1) * LANES] = out_t.T.astype(BF16)


def _attention(k, fk, qt, vt, ft):
    b, s, _ = k.shape
    pp = ATTN_PAIRS
    w = pp * LANES
    kern = functools.partial(_attn_kernel, tq=TQ_ATTN, tk=TK_ATTN, seq=s, pairs=pp)
    return pl.pallas_call(
        kern,
        grid=(b, N_HEADS // 2 // pp),
        in_specs=[
            pl.BlockSpec((None, s, w), lambda bi, p: (bi, 0, p)),
            pl.BlockSpec((None, s, LANES), lambda bi, p: (bi, 0, 0)),
            pl.BlockSpec((None, w, s), lambda bi, p: (bi, p, 0)),
            pl.BlockSpec((None, w, s), lambda bi, p: (bi, p, 0)),
            pl.BlockSpec((None, pp, 2, s), lambda bi, p: (bi, p, 0, 0)),
        ],
        out_specs=pl.BlockSpec((None, s, w), lambda bi, p: (bi, 0, p)),
        out_shape=jax.ShapeDtypeStruct((b, s, ATTN_W), BF16),
        scratch_shapes=[
            pltpu.VMEM((pp, 2, TK_ATTN, 2 * TQ_ATTN), F32),
            pltpu.VMEM((pp, 2, TK_ATTN, 2 * TQ_ATTN), BF16),
        ],
        compiler_params=pltpu.CompilerParams(
            dimension_semantics=("arbitrary", "arbitrary"), vmem_limit_bytes=VMEM_LIMIT),
        name="fox_attention",
    )(k, fk, qt, vt, ft)


def _ffn_kernel(x_ref, attn_ref, pool_ref, mod_ref, g_ref, wo_ref, wup_ref, cw_ref, cb_ref,
                wdn_ref, fg_ref, o_ref, h_ref, act_ref, carry_ref, *, tm, layer, final):
    i = pl.program_id(1)

    @pl.when(i == 0)
    def _():
        carry_ref[...] = jnp.zeros_like(carry_ref)

    mixed = jnp.concatenate([attn_ref[...], pool_ref[...]], axis=1)
    y = jnp.dot(mixed, wo_ref[...], preferred_element_type=F32)
    seq = pl.ds(pl.program_id(0), 1)
    x1 = x_ref[...] + mod_ref[2, seq, :] * y
    o_ref[...] = x1
    ms = jnp.mean(x1 * x1, axis=-1, keepdims=True)
    scale = g_ref[layer:layer + 1, :] * (1.0 + mod_ref[4, seq, :])
    h_ref[...] = ((x1 * lax.rsqrt(ms + EPS)) * scale + mod_ref[3, seq, :]).astype(BF16)

    def conv_branch(c0):
        cols = slice(c0, c0 + FF_CHUNK)
        a = jnp.dot(h_ref[...], wup_ref[:, cols], preferred_element_type=F32)
        ext = jnp.concatenate([carry_ref[:, cols], a], axis=0)
        carry_ref[:, cols] = a[tm - SUBLANES:tm, :]
        a1 = pltpu.roll(ext, 1, axis=0)[SUBLANES:]
        a2 = pltpu.roll(ext, 2, axis=0)[SUBLANES:]
        return (cb_ref[layer:layer + 1, cols] + a2 * cw_ref[layer, 0:1, cols]
                + a1 * cw_ref[layer, 1:2, cols] + a * cw_ref[layer, 2:3, cols])

    for c in range(N_FF_CHUNKS):
        gate = conv_branch(c * FF_CHUNK)
        val = conv_branch(D_FF + c * FF_CHUNK)
        half_gv = (0.5 * gate) * val
        act_ref[:, c * FF_CHUNK:(c + 1) * FF_CHUNK] = (
            half_gv + half_gv * jnp.tanh(0.5 * gate)).astype(BF16)

    y2 = jnp.dot(act_ref[...], wdn_ref[...], preferred_element_type=F32)
    x2 = o_ref[...] + mod_ref[5, seq, :] * y2
    if final:
        ms2 = jnp.mean(x2 * x2, axis=-1, keepdims=True)
        x2 = (x2 * lax.rsqrt(ms2 + EPS)) * fg_ref[...]
    o_ref[...] = x2


def _ffn(x, attn, pool, mod, g2, wo, wup, cw, cb, wdn, fg, layer, final):
    b, s, d = x.shape
    tm = TM_FFN
    kern = functools.partial(_ffn_kernel, tm=tm, layer=layer, final=final)
    row = lambda bi, i: (bi, i, 0)
    const2 = lambda bi, i: (0, 0)
    once = pl.Buffered(1)
    return pl.pallas_call(
        kern,
        grid=(b, s // tm),
        in_specs=[
            pl.BlockSpec((None, tm, d), row),
            pl.BlockSpec((None, tm, ATTN_W), row),
            pl.BlockSpec((None, tm, POOL_W), row),
            pl.BlockSpec((None, N_MOD, b, d), lambda bi, i: (layer, 0, 0, 0)),
            pl.BlockSpec(g2.shape, const2),
            pl.BlockSpec((d, d), const2, pipeline_mode=once),
            pl.BlockSpec((d, 2 * D_FF), const2, pipeline_mode=once),
            pl.BlockSpec(cw.shape, lambda bi, i: (0, 0, 0)),
            pl.BlockSpec(cb.shape, const2),
            pl.BlockSpec((D_FF, d), const2, pipeline_mode=once),
            pl.BlockSpec((1, d), const2),
        ],
        out_specs=pl.BlockSpec((None, tm, d), row),
        out_shape=jax.ShapeDtypeStruct((b, s, d), F32),
        scratch_shapes=[
            pltpu.VMEM((tm, d), BF16),
            pltpu.VMEM((tm, D_FF), BF16),
            pltpu.VMEM((SUBLANES, 2 * D_FF), F32),
        ],
        compiler_params=pltpu.CompilerParams(
            dimension_semantics=("arbitrary", "arbitrary"), vmem_limit_bytes=VMEM_LIMIT),
        name="outproj_convffn",
    )(x, attn, pool, mod, g2, wo, wup, cw, cb, wdn, fg)


def kernel(x, c, mod_w, mod_b, norm1_g, norm2_g, w_in, b_f, pool_w, pool_scale, w_out, ffn_up,
           ffn_conv_w, ffn_conv_b, ffn_down, final_g):
    b, s, d = x.shape
    depth = mod_w.shape[0]
    assert (d, s % TM_PREMIX, s % TQ_ATTN, s % TM_FFN) == (D_MODEL, 0, 0, 0)

    mod = _modulation(c, mod_w, mod_b)
    fg = final_g.reshape(1, d)
    bf = b_f.reshape(depth, N_HEADS, 1)
    w_in_t = jnp.swapaxes(w_in, 1, 2)

    for l in range(depth):
        k, fk, qt, vt, ft, pool, wo, wup, wdn = _premix(
            x, mod, norm1_g, w_in_t, l, bf, pool_w, pool_scale, w_out, ffn_up, ffn_down)
        attn = _attention(k, fk, qt, vt, ft)
        x = _ffn(x, attn, pool, mod, norm2_g, wo, wup, ffn_conv_w, ffn_conv_b, wdn, fg,
                 layer=l, final=(l == depth - 1))
    return x
```

```python
import functools

import jax
import jax.numpy as jnp
from jax import lax
from jax.experimental import pallas as pl
from jax.experimental.pallas import tpu as pltpu

F32 = jnp.float32
BF16 = jnp.bfloat16

D_MODEL = 1024
N_HEADS = 8
HEAD_DIM = 64
ATTN_W = N_HEADS * HEAD_DIM
POOL_WINDOWS = (2, 4, 8, 16)
POOL_C = 128
POOL_W = POOL_C * len(POOL_WINDOWS)
D_FF = 2816
CONV_W = 3
N_MOD = 6
MOD_VECS_PER_STEP = 2
EPS = 1e-6

LANES = 128
SUBLANES = 8
BF16_SUBLANES = 16
HALO = 16
TRI = 256
FF_CHUNK = 256
N_FF_CHUNKS = D_FF // FF_CHUNK
F_PIECES = 3
WT_F_ROWS = 16
ONES_LANE = 32
NEG_BIG = -1e30
LOG2E = 1.4426950408889634
V_ONES_ROWS = 16

TM_PREMIX = 1024
PREMIX_ROW_BLOCKS = 2
TQ_ATTN = 512
TK_ATTN = 256
ATTN_PAIRS = 1
TM_FFN = 1024
HEAD_ROW_BLOCKS = 4
V7X_VMEM_BYTES = 64 * 1024 * 1024
VMEM_LIMIT = V7X_VMEM_BYTES - 8 * 1024 * 1024


def _split3(x):
    hi = x.astype(BF16).astype(F32)
    r = x - hi
    mid = r.astype(BF16).astype(F32)
    lo = (r - mid).astype(BF16).astype(F32)
    return hi, mid, lo


def _mod_kernel(c_ref, w_ref, b_ref, o_ref):
    c = c_ref[...]
    ca = (c * jax.nn.sigmoid(c)).astype(BF16)
    res = jnp.dot(ca, w_ref[...].astype(BF16), preferred_element_type=F32) + b_ref[...]
    d = c.shape[1]
    for v in range(o_ref.shape[0]):
        o_ref[v] = res[:, v * d:(v + 1) * d]


def _modulation(c, mod_w, mod_b):
    depth, d, n = mod_w.shape
    b = c.shape[0]
    cols = MOD_VECS_PER_STEP * d
    return pl.pallas_call(
        _mod_kernel,
        grid=(depth, n // cols),
        in_specs=[
            pl.BlockSpec((b, d), lambda l, j: (0, 0)),
            pl.BlockSpec((None, d, cols), lambda l, j: (l, 0, j)),
            pl.BlockSpec((None, 1, cols), lambda l, j: (l, 0, j)),
        ],
        out_specs=pl.BlockSpec((None, MOD_VECS_PER_STEP, b, d), lambda l, j: (l, j, 0, 0)),
        out_shape=jax.ShapeDtypeStruct((depth, n // d, b, d), F32),
        compiler_params=pltpu.CompilerParams(
            dimension_semantics=("arbitrary", "arbitrary"), vmem_limit_bytes=VMEM_LIMIT),
        name="modulation",
    )(c, mod_w, mod_b.reshape(depth, 1, n))


def _premix_kernel(x_ref, mod_ref, g_ref, w_ref, bf_ref, pw_ref, ps_ref,
                   wo_ref, wup_ref, wdn_ref,
                   k_ref, fk_ref, qt_ref, vt_ref, ft_ref, pool_ref, wo_bf_ref, wup_bf_ref, wdn_bf_ref,
                   carry_ref, ubuf_ref, wku_ref, wt_ref, tri_ref, *, tm, layer):
    i = pl.program_id(1)

    @pl.when((pl.program_id(0) == 0) & (i == 0))
    def _():
        a0, a1, a2, a3 = ATTN_W, 2 * ATTN_W, 3 * ATTN_W, 3 * ATTN_W + N_HEADS
        wku_ref[:ATTN_W, :] = w_ref[a0:a1, :].astype(BF16)
        wku_ref[ATTN_W:, :] = w_ref[a3:, :].astype(BF16)
        wt_ref[0:WT_F_ROWS, :] = jnp.concatenate(
            [w_ref[a2:a3, :], jnp.zeros((WT_F_ROWS - N_HEADS, w_ref.shape[1]), F32)], axis=0).astype(BF16)
        wt_ref[WT_F_ROWS:WT_F_ROWS + ATTN_W, :] = (
            w_ref[0:a0, :] * (HEAD_DIM ** -0.5 * LOG2E)).astype(BF16)
        wt_ref[WT_F_ROWS + ATTN_W:, :] = w_ref[a1:a2, :].astype(BF16)
        rr = lax.broadcasted_iota(jnp.int32, (TRI, TRI), 0)
        cc = lax.broadcasted_iota(jnp.int32, (TRI, TRI), 1)
        tri_ref[...] = jnp.where(rr <= cc, 1.0, 0.0).astype(BF16)

    @pl.when(i == 0)
    def _():
        carry_ref[...] = jnp.zeros_like(carry_ref)
        ubuf_ref[...] = jnp.zeros_like(ubuf_ref)

    seq = pl.ds(pl.program_id(0), 1)
    scale = g_ref[layer:layer + 1, :] * (1.0 + mod_ref[1, seq, :])
    nt = (((1,), (1,)), ((), ()))
    u_blocks, f_blocks = [], []
    for r0 in range(0, tm, tm // PREMIX_ROW_BLOCKS):
        rows = slice(r0, r0 + tm // PREMIX_ROW_BLOCKS)
        x = x_ref[rows, :]
        ms = jnp.mean(x * x, axis=-1, keepdims=True)
        hb = ((x * lax.rsqrt(ms + EPS)) * scale + mod_ref[0, seq, :]).astype(BF16)
        ku = lax.dot_general(hb, wku_ref[...], nt, preferred_element_type=F32)
        k_ref[rows, :] = ku[:, :ATTN_W].astype(BF16)
        u_blocks.append(ku[:, ATTN_W:])
        zt = lax.dot_general(wt_ref[...], hb, nt,
                             preferred_element_type=F32)
        qt_ref[:, rows] = zt[WT_F_ROWS:WT_F_ROWS + ATTN_W].astype(BF16)
        vt_ref[:, rows] = zt[WT_F_ROWS + ATTN_W:].astype(BF16)
        f_blocks.append(zt[0:N_HEADS])
    u = jnp.concatenate(u_blocks, axis=0)

    wo_bf_ref[...] = wo_ref[...].astype(BF16)
    wup_bf_ref[...] = wup_ref[...].astype(BF16)
    wdn_bf_ref[...] = wdn_ref[...].astype(BF16)

    fl = jnp.concatenate(f_blocks, axis=1) + bf_ref[...]
    logf = jnp.minimum(fl, 0.0) - jnp.log1p(jnp.exp(-jnp.abs(fl)))
    pieces = jnp.concatenate(list(_split3(logf)) + [jnp.zeros((SUBLANES, tm), F32)],
                             axis=0).astype(BF16)
    blocks, offset = [], carry_ref[:, 0:1]
    for c0 in range(0, tm, TRI):
        cs = jnp.dot(pieces[:, c0:c0 + TRI], tri_ref[...], preferred_element_type=F32)
        f_blk = (cs[0:8] + cs[8:16] + cs[16:24]) + offset
        offset = f_blk[:, TRI - 1:TRI]
        blocks.append(f_blk)
    f_cum = jnp.concatenate(blocks, axis=1)
    carry_ref[...] = jnp.broadcast_to(offset, carry_ref.shape)
    f_log2 = f_cum * LOG2E
    for pair in range(N_HEADS // 2):
        ft_ref[pair] = f_log2[2 * pair:2 * pair + 2]

    blk = jnp.concatenate(list(_split3(-f_log2))
                          + [jnp.zeros((LANES - F_PIECES * N_HEADS, tm), F32)], axis=0)
    blk_t = blk.T
    lane = lax.broadcasted_iota(jnp.int32, (tm, LANES), 1)
    blk_t = jnp.where((lane >= ONES_LANE) & (lane < ONES_LANE + F_PIECES), 1.0, blk_t)
    fk_ref[...] = blk_t.astype(BF16)

    pos = i * tm + lax.broadcasted_iota(jnp.int32, (tm, POOL_C), 0)
    for g, w in enumerate(POOL_WINDOWS):
        c0 = g * POOL_C
        ug = u[:, c0:c0 + POOL_C]
        ws = jnp.concatenate([ubuf_ref[:, c0:c0 + POOL_C], ug], axis=0)
        shift = 1
        while shift < w:
            ws = ws + pltpu.roll(ws, shift, axis=0)
            shift *= 2
        cnt = jnp.minimum(pos + 1, w).astype(F32)
        d = ws[HALO:] / cnt - ug
        pg = jnp.dot(d.astype(BF16), pw_ref[g].astype(BF16), preferred_element_type=F32)
        pool_ref[:, c0:c0 + POOL_C] = (pg * ps_ref[layer:layer + 1, c0:c0 + POOL_C]).astype(BF16)
    ubuf_ref[...] = u[tm - HALO:, :]


def _premix(x, mod, g1, w_in_t, layer, b_f, pool_w, pool_scale, w_out, ffn_up, ffn_down):
    b, s, d = x.shape
    tm = TM_PREMIX
    once = pl.Buffered(1)
    n_i = s // tm
    n_steps = b * n_i

    def cast_specs(w):
        rows, cols = w.shape[1:]
        share = 1
        while rows % (n_steps // share) or (rows // (n_steps // share)) % BF16_SUBLANES:
            share *= 2
        slab = rows // (n_steps // share)
        return (pl.BlockSpec((None, slab, cols), lambda bi, i: (layer, (bi * n_i + i) // share, 0)),
                pl.BlockSpec((slab, cols), lambda bi, i: ((bi * n_i + i) // share, 0)),
                jax.ShapeDtypeStruct((rows, cols), BF16))

    cast_in, cast_out, cast_shape = zip(*(cast_specs(w) for w in (w_out, ffn_up, ffn_down)))
    kern = functools.partial(_premix_kernel, tm=tm, layer=layer)
    row = lambda bi, i: (bi, i, 0)
    col = lambda bi, i: (bi, 0, i)
    const2 = lambda bi, i: (0, 0)
    per_layer = lambda bi, i: (layer, 0, 0)
    return pl.pallas_call(
        kern,
        grid=(b, s // tm),
        in_specs=[
            pl.BlockSpec((None, tm, d), row),
            pl.BlockSpec((None, N_MOD, b, d), lambda bi, i: (layer, 0, 0, 0)),
            pl.BlockSpec(g1.shape, const2),
            pl.BlockSpec((None, w_in_t.shape[1], d), per_layer, pipeline_mode=once),
            pl.BlockSpec((None, N_HEADS, 1), per_layer),
            pl.BlockSpec((None,) + pool_w.shape[1:], lambda bi, i: (layer, 0, 0, 0)),
            pl.BlockSpec(pool_scale.shape, const2),
            *cast_in,
        ],
        out_specs=[
            pl.BlockSpec((None, tm, ATTN_W), row),
            pl.BlockSpec((None, tm, LANES), row),
            pl.BlockSpec((None, ATTN_W, tm), col),
            pl.BlockSpec((None, ATTN_W, tm), col),
            pl.BlockSpec((None, N_HEADS // 2, 2, tm), lambda bi, i: (bi, 0, 0, i)),
            pl.BlockSpec((None, tm, POOL_W), row),
            *cast_out,
        ],
        out_shape=[
            jax.ShapeDtypeStruct((b, s, ATTN_W), BF16),
            jax.ShapeDtypeStruct((b, s, LANES), BF16),
            jax.ShapeDtypeStruct((b, ATTN_W, s), BF16),
            jax.ShapeDtypeStruct((b, ATTN_W, s), BF16),
            jax.ShapeDtypeStruct((b, N_HEADS // 2, 2, s), F32),
            jax.ShapeDtypeStruct((b, s, POOL_W), BF16),
            *cast_shape,
        ],
        scratch_shapes=[
            pltpu.VMEM((N_HEADS, LANES), F32),
            pltpu.VMEM((HALO, POOL_W), F32),
            pltpu.VMEM((ATTN_W + POOL_W, d), BF16),
            pltpu.VMEM((2 * ATTN_W + WT_F_ROWS, d), BF16),
            pltpu.VMEM((TRI, TRI), BF16),
        ],
        compiler_params=pltpu.CompilerParams(
            dimension_semantics=("arbitrary", "arbitrary"), vmem_limit_bytes=VMEM_LIMIT),
        name="premix",
    )(x, mod, g1, w_in_t, b_f, pool_w, pool_scale, w_out, ffn_up, ffn_down)


def _attn_kernel(k_ref, fk_ref, qt_ref, vt_ref, ft_ref, o_ref, s_scr, p_scr, *, tq, tk, seq, pairs):
    assert tq % tk == 0
    first_pair = pl.program_id(1) * pairs
    row = lax.broadcasted_iota(jnp.int32, (LANES, tq), 0)
    key_pos = lax.broadcasted_iota(jnp.int32, (tk, tq), 0)
    qry_pos = lax.broadcasted_iota(jnp.int32, (tk, tq), 1)
    ones_rows = jnp.ones((V_ONES_ROWS, tk), BF16)

    def scores(pr, t, q_all):
        r0 = t * tk
        k_aug = jnp.concatenate([k_ref[r0:r0 + tk, pr * LANES:(pr + 1) * LANES],
                                 fk_ref[r0:r0 + tk, :]], axis=1)
        s_scr[pr, t % 2] = jnp.dot(k_aug, q_all, preferred_element_type=F32)

    def softmax(pr, t, ms, mask):
        new_ms, alphas = [], []
        for hh in range(2):
            s = s_scr[pr, t % 2, :, hh * tq:(hh + 1) * tq]
            if mask is not None:
                s = jnp.where(mask, s, -jnp.inf)
            m_new = jnp.maximum(ms[hh], jnp.max(s, axis=0, keepdims=True))
            alphas.append(jnp.exp2(ms[hh] - m_new))
            new_ms.append(m_new)
            p_scr[pr, t % 2, :, hh * tq:(hh + 1) * tq] = jnp.exp2(s - m_new).astype(BF16)
        return new_ms, alphas

    def accumulate(pr, t, alphas, accs):
        r0 = t * tk
        out = []
        for hh in range(2):
            v0 = pr * LANES + hh * HEAD_DIM
            v_aug = jnp.concatenate([vt_ref[v0:v0 + HEAD_DIM, r0:r0 + tk], ones_rows], axis=0)
            pv = jnp.dot(v_aug, p_scr[pr, t % 2, :, hh * tq:(hh + 1) * tq],
                         preferred_element_type=F32)
            out.append(alphas[hh] * accs[hh] + pv)
        return out

    def build_q(pr, c0):
        qt2 = qt_ref[pr * LANES:(pr + 1) * LANES, c0:c0 + tq]
        q_augs = []
        for hh in range(2):
            head = 2 * (first_pair + pr) + hh
            q_h = jnp.where((row >= hh * HEAD_DIM) & (row < (hh + 1) * HEAD_DIM), qt2,
                            jnp.zeros_like(qt2))
            pieces = _split3(ft_ref[pr, hh:hh + 1, c0:c0 + tq])
            fblk = jnp.where((row < F_PIECES * N_HEADS) & (row % N_HEADS == head), 1.0, 0.0)
            for j in range(F_PIECES):
                fblk = jnp.where(row == ONES_LANE + j, pieces[j], fblk)
            q_augs.append(jnp.concatenate([q_h, fblk.astype(BF16)], axis=0))
        return jnp.concatenate(q_augs, axis=1)

    prs = range(pairs)
    for qb in range(seq // tq):
        c0 = qb * tq
        q_all = [build_q(pr, c0) for pr in prs]
        n_full = (tq // tk) * qb
        n_tiles = n_full + tq // tk
        ms = [[jnp.full((1, tq), NEG_BIG, F32) for _ in range(2)] for _ in prs]
        accs = [[jnp.zeros((HEAD_DIM + V_ONES_ROWS, tq), F32) for _ in range(2)] for _ in prs]
        pending = None
        for pr in prs:
            scores(pr, 0, q_all[pr])
        for t in range(n_tiles):
            if t + 1 < n_tiles:
                for pr in prs:
                    scores(pr, t + 1, q_all[pr])
            mask = None if t < n_full else (key_pos + (t - n_full) * tk <= qry_pos)
            alphas = [None] * pairs
            for pr in prs:
                ms[pr], alphas[pr] = softmax(pr, t, ms[pr], mask)
            if pending is not None:
                for pr in prs:
                    accs[pr] = accumulate(pr, pending[0], pending[1][pr], accs[pr])
            pending = (t, alphas)
        for pr in prs:
            accs[pr] = accumulate(pr, pending[0], pending[1][pr], accs[pr])
            out_t = jnp.concatenate(
                [a[:HEAD_DIM] / a[HEAD_DIM:HEAD_DIM + 1] for a in accs[pr]], axis=0)
            o_ref[c0:c0 + tq, pr * LANES:(pr + 1) * LANES] = out_t.T.astype(BF16)


def _attention(k, fk, qt, vt, ft):
    b, s, _ = k.shape
    pp = ATTN_PAIRS
    w = pp * LANES
    kern = functools.partial(_attn_kernel, tq=TQ_ATTN, tk=TK_ATTN, seq=s, pairs=pp)
    return pl.pallas_call(
        kern,
        grid=(b, N_HEADS // 2 // pp),
        in_specs=[
            pl.BlockSpec((None, s, w), lambda bi, p: (bi, 0, p)),
            pl.BlockSpec((None, s, LANES), lambda bi, p: (bi, 0, 0)),
            pl.BlockSpec((None, w, s), lambda bi, p: (bi, p, 0)),
            pl.BlockSpec((None, w, s), lambda bi, p: (bi, p, 0)),
            pl.BlockSpec((None, pp, 2, s), lambda bi, p: (bi, p, 0, 0)),
        ],
        out_specs=pl.BlockSpec((None, s, w), lambda bi, p: (bi, 0, p)),
        out_shape=jax.ShapeDtypeStruct((b, s, ATTN_W), BF16),
        scratch_shapes=[
            pltpu.VMEM((pp, 2, TK_ATTN, 2 * TQ_ATTN), F32),
            pltpu.VMEM((pp, 2, TK_ATTN, 2 * TQ_ATTN), BF16),
        ],
        compiler_params=pltpu.CompilerParams(
            dimension_semantics=("arbitrary", "arbitrary"), vmem_limit_bytes=VMEM_LIMIT),
        name="fox_attention",
    )(k, fk, qt, vt, ft)


def _ffn_kernel(x_ref, attn_ref, pool_ref, mod_ref, g_ref, wo_ref, wup_ref, cw_ref, cb_ref,
                wdn_ref, fg_ref, o_ref, h_ref, act_ref, carry_ref, *, tm, layer, final):
    i = pl.program_id(1)

    @pl.when(i == 0)
    def _():
        carry_ref[...] = jnp.zeros_like(carry_ref)

    seq = pl.ds(pl.program_id(0), 1)
    scale = g_ref[layer:layer + 1, :] * (1.0 + mod_ref[4, seq, :])
    for r0 in range(0, tm, tm // HEAD_ROW_BLOCKS):
        rows = slice(r0, r0 + tm // HEAD_ROW_BLOCKS)
        mixed = jnp.concatenate([attn_ref[rows, :], pool_ref[rows, :]], axis=1)
        y = jnp.dot(mixed, wo_ref[...], preferred_element_type=F32)
        x1 = x_ref[rows, :] + mod_ref[2, seq, :] * y
        o_ref[rows, :] = x1
        ms = jnp.mean(x1 * x1, axis=-1, keepdims=True)
        h_ref[rows, :] = ((x1 * lax.rsqrt(ms + EPS)) * scale + mod_ref[3, seq, :]).astype(BF16)

    def conv_branch(c0):
        cols = slice(c0, c0 + FF_CHUNK)
        a = jnp.dot(h_ref[...], wup_ref[:, cols], preferred_element_type=F32)
        ext = jnp.concatenate([carry_ref[:, cols], a], axis=0)
        carry_ref[:, cols] = a[tm - SUBLANES:tm, :]
        a1 = pltpu.roll(ext, 1, axis=0)[SUBLANES:]
        a2 = pltpu.roll(ext, 2, axis=0)[SUBLANES:]
        return (cb_ref[layer:layer + 1, cols] + a2 * cw_ref[layer, 0:1, cols]
                + a1 * cw_ref[layer, 1:2, cols] + a * cw_ref[layer, 2:3, cols])

    for c in range(N_FF_CHUNKS):
        gate = conv_branch(c * FF_CHUNK)
        val = conv_branch(D_FF + c * FF_CHUNK)
        half_gv = (0.5 * gate) * val
        act_ref[:, c * FF_CHUNK:(c + 1) * FF_CHUNK] = (
            half_gv + half_gv * jnp.tanh(0.5 * gate)).astype(BF16)

    y2 = jnp.dot(act_ref[...], wdn_ref[...], preferred_element_type=F32)
    x2 = o_ref[...] + mod_ref[5, seq, :] * y2
    if final:
        ms2 = jnp.mean(x2 * x2, axis=-1, keepdims=True)
        x2 = (x2 * lax.rsqrt(ms2 + EPS)) * fg_ref[...]
    o_ref[...] = x2


def _ffn(x, attn, pool, mod, g2, wo, wup, cw, cb, wdn, fg, layer, final):
    b, s, d = x.shape
    tm = TM_FFN
    kern = functools.partial(_ffn_kernel, tm=tm, layer=layer, final=final)
    row = lambda bi, i: (bi, i, 0)
    const2 = lambda bi, i: (0, 0)
    once = pl.Buffered(1)
    return pl.pallas_call(
        kern,
        grid=(b, s // tm),
        in_specs=[
            pl.BlockSpec((None, tm, d), row),
            pl.BlockSpec((None, tm, ATTN_W), row),
            pl.BlockSpec((None, tm, POOL_W), row),
            pl.BlockSpec((None, N_MOD, b, d), lambda bi, i: (layer, 0, 0, 0)),
            pl.BlockSpec(g2.shape, const2),
            pl.BlockSpec((d, d), const2, pipeline_mode=once),
            pl.BlockSpec((d, 2 * D_FF), const2, pipeline_mode=once),
            pl.BlockSpec(cw.shape, lambda bi, i: (0, 0, 0)),
            pl.BlockSpec(cb.shape, const2),
            pl.BlockSpec((D_FF, d), const2, pipeline_mode=once),
            pl.BlockSpec((1, d), const2),
        ],
        out_specs=pl.BlockSpec((None, tm, d), row),
        out_shape=jax.ShapeDtypeStruct((b, s, d), F32),
        scratch_shapes=[
            pltpu.VMEM((tm, d), BF16),
            pltpu.VMEM((tm, D_FF), BF16),
            pltpu.VMEM((SUBLANES, 2 * D_FF), F32),
        ],
        compiler_params=pltpu.CompilerParams(
            dimension_semantics=("arbitrary", "arbitrary"), vmem_limit_bytes=VMEM_LIMIT),
        name="outproj_convffn",
    )(x, attn, pool, mod, g2, wo, wup, cw, cb, wdn, fg)


def kernel(x, c, mod_w, mod_b, norm1_g, norm2_g, w_in, b_f, pool_w, pool_scale, w_out, ffn_up,
           ffn_conv_w, ffn_conv_b, ffn_down, final_g):
    b, s, d = x.shape
    depth = mod_w.shape[0]
    assert (d, s % TM_PREMIX, s % TQ_ATTN, s % TM_FFN) == (D_MODEL, 0, 0, 0)

    mod = _modulation(c, mod_w, mod_b)
    fg = final_g.reshape(1, d)
    bf = b_f.reshape(depth, N_HEADS, 1)
    w_in_t = jnp.swapaxes(w_in, 1, 2)

    for l in range(depth):
        k, fk, qt, vt, ft, pool, wo, wup, wdn = _premix(
            x, mod, norm1_g, w_in_t, l, bf, pool_w, pool_scale, w_out, ffn_up, ffn_down)
        attn = _attention(k, fk, qt, vt, ft)
        x = _ffn(x, attn, pool, mod, norm2_g, wo, wup, ffn_conv_w, ffn_conv_b, wdn, fg,
                 layer=l, final=(l == depth - 1))
    return x
```

```python
import functools

import jax
import jax.numpy as jnp
from jax import lax
from jax.experimental import pallas as pl
from jax.experimental.pallas import tpu as pltpu

F32 = jnp.float32
BF16 = jnp.bfloat16

D_MODEL = 1024
N_HEADS = 8
HEAD_DIM = 64
ATTN_W = N_HEADS * HEAD_DIM
POOL_WINDOWS = (2, 4, 8, 16)
POOL_C = 128
POOL_W = POOL_C * len(POOL_WINDOWS)
D_FF = 2816
CONV_W = 3
N_MOD = 6
MOD_VECS_PER_STEP = 2
EPS = 1e-6

LANES = 128
SUBLANES = 8
BF16_SUBLANES = 16
HALO = 16
TRI = 256
FF_CHUNK = 256
N_FF_CHUNKS = D_FF // FF_CHUNK
F_PIECES = 3
WT_F_ROWS = 16
ONES_LANE = 32
NEG_BIG = -1e30
LOG2E = 1.4426950408889634
V_ONES_ROWS = 16

TM_PREMIX = 1024
PREMIX_ROW_BLOCKS = 2
TQ_ATTN = 512
TK_ATTN = 256
ATTN_PAIRS = 1
TM_FFN = 1024
HEAD_ROW_BLOCKS = 4
V7X_VMEM_BYTES = 64 * 1024 * 1024
VMEM_LIMIT = V7X_VMEM_BYTES - 8 * 1024 * 1024


def _split3(x):
    hi = x.astype(BF16).astype(F32)
    r = x - hi
    mid = r.astype(BF16).astype(F32)
    lo = (r - mid).astype(BF16).astype(F32)
    return hi, mid, lo


def _mod_kernel(c_ref, w_ref, b_ref, o_ref):
    c = c_ref[...]
    ca = (c * jax.nn.sigmoid(c)).astype(BF16)
    res = jnp.dot(ca, w_ref[...].astype(BF16), preferred_element_type=F32) + b_ref[...]
    d = c.shape[1]
    for v in range(o_ref.shape[0]):
        o_ref[v] = res[:, v * d:(v + 1) * d]


def _modulation(c, mod_w, mod_b):
    depth, d, n = mod_w.shape
    b = c.shape[0]
    cols = MOD_VECS_PER_STEP * d
    return pl.pallas_call(
        _mod_kernel,
        grid=(depth, n // cols),
        in_specs=[
            pl.BlockSpec((b, d), lambda l, j: (0, 0)),
            pl.BlockSpec((None, d, cols), lambda l, j: (l, 0, j)),
            pl.BlockSpec((None, 1, cols), lambda l, j: (l, 0, j)),
        ],
        out_specs=pl.BlockSpec((None, MOD_VECS_PER_STEP, b, d), lambda l, j: (l, j, 0, 0)),
        out_shape=jax.ShapeDtypeStruct((depth, n // d, b, d), F32),
        compiler_params=pltpu.CompilerParams(
            dimension_semantics=("arbitrary", "arbitrary"), vmem_limit_bytes=VMEM_LIMIT),
        name="modulation",
    )(c, mod_w, mod_b.reshape(depth, 1, n))


def _premix_kernel(x_ref, mod_ref, g_ref, w_ref, bf_ref, pw_ref, ps_ref,
                   wo_ref, wup_ref, wdn_ref,
                   k_ref, fk_ref, qt_ref, vt_ref, ft_ref, pool_ref, wo_bf_ref, wup_bf_ref, wdn_bf_ref,
                   carry_ref, ubuf_ref, wku_ref, wt_ref, tri_ref, *, tm, layer):
    i = pl.program_id(1)

    @pl.when((pl.program_id(0) == 0) & (i == 0))
    def _():
        a0, a1, a2, a3 = ATTN_W, 2 * ATTN_W, 3 * ATTN_W, 3 * ATTN_W + N_HEADS
        wku_ref[:ATTN_W, :] = w_ref[a0:a1, :].astype(BF16)
        wku_ref[ATTN_W:, :] = w_ref[a3:, :].astype(BF16)
        wt_ref[0:WT_F_ROWS, :] = jnp.concatenate(
            [w_ref[a2:a3, :], jnp.zeros((WT_F_ROWS - N_HEADS, w_ref.shape[1]), F32)], axis=0).astype(BF16)
        wt_ref[WT_F_ROWS:WT_F_ROWS + ATTN_W, :] = (
            w_ref[0:a0, :] * (HEAD_DIM ** -0.5 * LOG2E)).astype(BF16)
        wt_ref[WT_F_ROWS + ATTN_W:, :] = w_ref[a1:a2, :].astype(BF16)
        rr = lax.broadcasted_iota(jnp.int32, (TRI, TRI), 0)
        cc = lax.broadcasted_iota(jnp.int32, (TRI, TRI), 1)
        tri_ref[...] = jnp.where(rr <= cc, 1.0, 0.0).astype(BF16)

    @pl.when(i == 0)
    def _():
        carry_ref[...] = jnp.zeros_like(carry_ref)
        ubuf_ref[...] = jnp.zeros_like(ubuf_ref)

    seq = pl.ds(pl.program_id(0), 1)
    scale = g_ref[layer:layer + 1, :] * (1.0 + mod_ref[1, seq, :])
    nt = (((1,), (1,)), ((), ()))
    u_blocks, f_blocks = [], []
    for r0 in range(0, tm, tm // PREMIX_ROW_BLOCKS):
        rows = slice(r0, r0 + tm // PREMIX_ROW_BLOCKS)
        x = x_ref[rows, :]
        ms = jnp.mean(x * x, axis=-1, keepdims=True)
        hb = ((x * lax.rsqrt(ms + EPS)) * scale + mod_ref[0, seq, :]).astype(BF16)
        ku = lax.dot_general(hb, wku_ref[...], nt, preferred_element_type=F32)
        k_ref[rows, :] = ku[:, :ATTN_W].astype(BF16)
        u_blocks.append(ku[:, ATTN_W:])
        zt = lax.dot_general(wt_ref[...], hb, nt,
                             preferred_element_type=F32)
        qt_ref[:, rows] = zt[WT_F_ROWS:WT_F_ROWS + ATTN_W].astype(BF16)
        vt_ref[:, rows] = zt[WT_F_ROWS + ATTN_W:].astype(BF16)
        f_blocks.append(zt[0:N_HEADS])
    u = jnp.concatenate(u_blocks, axis=0)

    wo_bf_ref[...] = wo_ref[...].astype(BF16)
    wup_bf_ref[...] = wup_ref[...].astype(BF16)
    wdn_bf_ref[...] = wdn_ref[...].astype(BF16)

    fl = jnp.concatenate(f_blocks, axis=1) + bf_ref[...]
    logf = jnp.minimum(fl, 0.0) - jnp.log1p(jnp.exp(-jnp.abs(fl)))
    pieces = jnp.concatenate(list(_split3(logf)) + [jnp.zeros((SUBLANES, tm), F32)],
                             axis=0).astype(BF16)
    blocks, offset = [], carry_ref[:, 0:1]
    for c0 in range(0, tm, TRI):
        cs = jnp.dot(pieces[:, c0:c0 + TRI], tri_ref[...], preferred_element_type=F32)
        f_blk = (cs[0:8] + cs[8:16] + cs[16:24]) + offset
        offset = f_blk[:, TRI - 1:TRI]
        blocks.append(f_blk)
    f_cum = jnp.concatenate(blocks, axis=1)
    carry_ref[...] = jnp.broadcast_to(offset, carry_ref.shape)
    f_log2 = f_cum * LOG2E
    for pair in range(N_HEADS // 2):
        ft_ref[pair] = f_log2[2 * pair:2 * pair + 2]

    blk = jnp.concatenate(list(_split3(-f_log2))
                          + [jnp.zeros((LANES - F_PIECES * N_HEADS, tm), F32)], axis=0)
    blk_t = blk.T
    lane = lax.broadcasted_iota(jnp.int32, (tm, LANES), 1)
    blk_t = jnp.where((lane >= ONES_LANE) & (lane < ONES_LANE + F_PIECES), 1.0, blk_t)
    fk_ref[...] = blk_t.astype(BF16)

    pos = i * tm + lax.broadcasted_iota(jnp.int32, (tm, POOL_C), 0)
    for g, w in enumerate(POOL_WINDOWS):
        c0 = g * POOL_C
        ug = u[:, c0:c0 + POOL_C]
        ws = jnp.concatenate([ubuf_ref[:, c0:c0 + POOL_C], ug], axis=0)
        shift = 1
        while shift < w:
            ws = ws + pltpu.roll(ws, shift, axis=0)
            shift *= 2
        cnt = jnp.minimum(pos + 1, w).astype(F32)
        d = ws[HALO:] / cnt - ug
        pg = jnp.dot(d.astype(BF16), pw_ref[g].astype(BF16), preferred_element_type=F32)
        pool_ref[:, c0:c0 + POOL_C] = (pg * ps_ref[layer:layer + 1, c0:c0 + POOL_C]).astype(BF16)
    ubuf_ref[...] = u[tm - HALO:, :]


def _premix(x, mod, g1, w_in_t, layer, b_f, pool_w, pool_scale, w_out, ffn_up, ffn_down):
    b, s, d = x.shape
    tm = TM_PREMIX
    once = pl.Buffered(1)
    n_i = s // tm
    n_steps = b * n_i

    def cast_specs(w):
        rows, cols = w.shape[1:]
        share = 1
        while rows % (n_steps // share) or (rows // (n_steps // share)) % BF16_SUBLANES:
            share *= 2
        slab = rows // (n_steps // share)
        return (pl.BlockSpec((None, slab, cols), lambda bi, i: (layer, (bi * n_i + i) // share, 0)),
                pl.BlockSpec((slab, cols), lambda bi, i: ((bi * n_i + i) // share, 0)),
                jax.ShapeDtypeStruct((rows, cols), BF16))

    cast_in, cast_out, cast_shape = zip(*(cast_specs(w) for w in (w_out, ffn_up, ffn_down)))
    kern = functools.partial(_premix_kernel, tm=tm, layer=layer)
    row = lambda bi, i: (bi, i, 0)
    col = lambda bi, i: (bi, 0, i)
    const2 = lambda bi, i: (0, 0)
    per_layer = lambda bi, i: (layer, 0, 0)
    return pl.pallas_call(
        kern,
        grid=(b, s // tm),
        in_specs=[
            pl.BlockSpec((None, tm, d), row),
            pl.BlockSpec((None, N_MOD, b, d), lambda bi, i: (layer, 0, 0, 0)),
            pl.BlockSpec(g1.shape, const2),
            pl.BlockSpec((None, w_in_t.shape[1], d), per_layer, pipeline_mode=once),
            pl.BlockSpec((None, N_HEADS, 1), per_layer),
            pl.BlockSpec((None,) + pool_w.shape[1:], lambda bi, i: (layer, 0, 0, 0)),
            pl.BlockSpec(pool_scale.shape, const2),
            *cast_in,
        ],
        out_specs=[
            pl.BlockSpec((None, tm, ATTN_W), row),
            pl.BlockSpec((None, tm, LANES), row),
            pl.BlockSpec((None, ATTN_W, tm), col),
            pl.BlockSpec((None, ATTN_W, tm), col),
            pl.BlockSpec((None, N_HEADS // 2, 2, tm), lambda bi, i: (bi, 0, 0, i)),
            pl.BlockSpec((None, tm, POOL_W), row),
            *cast_out,
        ],
        out_shape=[
            jax.ShapeDtypeStruct((b, s, ATTN_W), BF16),
            jax.ShapeDtypeStruct((b, s, LANES), BF16),
            jax.ShapeDtypeStruct((b, ATTN_W, s), BF16),
            jax.ShapeDtypeStruct((b, ATTN_W, s), BF16),
            jax.ShapeDtypeStruct((b, N_HEADS // 2, 2, s), F32),
            jax.ShapeDtypeStruct((b, s, POOL_W), BF16),
            *cast_shape,
        ],
        scratch_shapes=[
            pltpu.VMEM((N_HEADS, LANES), F32),
            pltpu.VMEM((HALO, POOL_W), F32),
            pltpu.VMEM((ATTN_W + POOL_W, d), BF16),
            pltpu.VMEM((2 * ATTN_W + WT_F_ROWS, d), BF16),
            pltpu.VMEM((TRI, TRI), BF16),
        ],
        compiler_params=pltpu.CompilerParams(
            dimension_semantics=("arbitrary", "arbitrary"), vmem_limit_bytes=VMEM_LIMIT),
        name="premix",
    )(x, mod, g1, w_in_t, b_f, pool_w, pool_scale, w_out, ffn_up, ffn_down)


def _attn_kernel(k_ref, fk_ref, qt_ref, vt_ref, ft_ref, o_ref, s_scr, p_scr, *, tq, tk, seq, pairs):
    assert tq % tk == 0
    first_pair = pl.program_id(1) * pairs
    row = lax.broadcasted_iota(jnp.int32, (LANES, tq), 0)
    key_pos = lax.broadcasted_iota(jnp.int32, (tk, tq), 0)
    qry_pos = lax.broadcasted_iota(jnp.int32, (tk, tq), 1)
    ones_rows = jnp.ones((V_ONES_ROWS, tk), BF16)

    def scores(pr, t, q_all):
        r0 = t * tk
        k_aug = jnp.concatenate([k_ref[r0:r0 + tk, pr * LANES:(pr + 1) * LANES],
                                 fk_ref[r0:r0 + tk, :]], axis=1)
        s_scr[pr, t % 2] = jnp.dot(k_aug, q_all, preferred_element_type=F32)

    def softmax(pr, t, ms, mask):
        new_ms, alphas = [], []
        for hh in range(2):
            s = s_scr[pr, t % 2, :, hh * tq:(hh + 1) * tq]
            if mask is not None:
                s = jnp.where(mask, s, -jnp.inf)
            m_new = jnp.maximum(ms[hh], jnp.max(s, axis=0, keepdims=True))
            alphas.append(jnp.exp2(ms[hh] - m_new))
            new_ms.append(m_new)
            p_scr[pr, t % 2, :, hh * tq:(hh + 1) * tq] = jnp.exp2(s - m_new).astype(BF16)
        return new_ms, alphas

    def accumulate(pr, t, alphas, accs):
        r0 = t * tk
        out = []
        for hh in range(2):
            v0 = pr * LANES + hh * HEAD_DIM
            v_aug = jnp.concatenate([vt_ref[v0:v0 + HEAD_DIM, r0:r0 + tk], ones_rows], axis=0)
            pv = jnp.dot(v_aug, p_scr[pr, t % 2, :, hh * tq:(hh + 1) * tq],
                         preferred_element_type=F32)
            out.append(alphas[hh] * accs[hh] + pv)
        return out

    def build_q(pr, c0):
        qt2 = qt_ref[pr * LANES:(pr + 1) * LANES, c0:c0 + tq]
        q_augs = []
        for hh in range(2):
            head = 2 * (first_pair + pr) + hh
            q_h = jnp.where((row >= hh * HEAD_DIM) & (row < (hh + 1) * HEAD_DIM), qt2,
                            jnp.zeros_like(qt2))
            pieces = _split3(ft_ref[pr, hh:hh + 1, c0:c0 + tq])
            fblk = jnp.where((row < F_PIECES * N_HEADS) & (row % N_HEADS == head), 1.0, 0.0)
            for j in range(F_PIECES):
                fblk = jnp.where(row == ONES_LANE + j, pieces[j], fblk)
            q_augs.append(jnp.concatenate([q_h, fblk.astype(BF16)], axis=0))
        return jnp.concatenate(q_augs, axis=1)

    prs = range(pairs)
    for qb in range(seq // tq):
        c0 = qb * tq
        q_all = [build_q(pr, c0) for pr in prs]
        n_full = (tq // tk) * qb
        n_tiles = n_full + tq // tk
        ms = [[jnp.full((1, tq), NEG_BIG, F32) for _ in range(2)] for _ in prs]
        accs = [[jnp.zeros((HEAD_DIM + V_ONES_ROWS, tq), F32) for _ in range(2)] for _ in prs]
        pending = None
        for pr in prs:
            scores(pr, 0, q_all[pr])
        for t in range(n_tiles):
            if t + 1 < n_tiles:
                for pr in prs:
                    scores(pr, t + 1, q_all[pr])
            mask = None if t < n_full else (key_pos + (t - n_full) * tk <= qry_pos)
            alphas = [None] * pairs
            for pr in prs:
                ms[pr], alphas[pr] = softmax(pr, t, ms[pr], mask)
            if pending is not None:
                for pr in prs:
                    accs[pr] = accumulate(pr, pending[0], pending[1][pr], accs[pr])
            pending = (t, alphas)
        for pr in prs:
            accs[pr] = accumulate(pr, pending[0], pending[1][pr], accs[pr])
            out_t = jnp.concatenate(
                [a[:HEAD_DIM] / a[HEAD_DIM:HEAD_DIM + 1] for a in accs[pr]], axis=0)
            o_ref[c0:c0 + tq, pr * LANES:(pr + 1) * LANES] = out_t.T.astype(BF16)


def _attention(k, fk, qt, vt, ft):
    b, s, _ = k.shape
    pp = ATTN_PAIRS
    w = pp * LANES
    kern = functools.partial(_attn_kernel, tq=TQ_ATTN, tk=TK_ATTN, seq=s, pairs=pp)
    return pl.pallas_call(
        kern,
        grid=(b, N_HEADS // 2 // pp),
        in_specs=[
            pl.BlockSpec((None, s, w), lambda bi, p: (bi, 0, p)),
            pl.BlockSpec((None, s, LANES), lambda bi, p: (bi, 0, 0)),
            pl.BlockSpec((None, w, s), lambda bi, p: (bi, p, 0)),
            pl.BlockSpec((None, w, s), lambda bi, p: (bi, p, 0)),
            pl.BlockSpec((None, pp, 2, s), lambda bi, p: (bi, p, 0, 0)),
        ],
        out_specs=pl.BlockSpec((None, s, w), lambda bi, p: (bi, 0, p)),
        out_shape=jax.ShapeDtypeStruct((b, s, ATTN_W), BF16),
        scratch_shapes=[
            pltpu.VMEM((pp, 2, TK_ATTN, 2 * TQ_ATTN), F32),
            pltpu.VMEM((pp, 2, TK_ATTN, 2 * TQ_ATTN), BF16),
        ],
        compiler_params=pltpu.CompilerParams(
            dimension_semantics=("arbitrary", "arbitrary"), vmem_limit_bytes=VMEM_LIMIT),
        name="fox_attention",
    )(k, fk, qt, vt, ft)


def _ffn_kernel(x_ref, attn_ref, pool_ref, mod_ref, g_ref, wo_ref, wup_ref, cw_ref, cb_ref,
                wdn_ref, fg_ref, o_ref, h_ref, act_ref, carry_ref, *, tm, layer, final):
    i = pl.program_id(1)

    @pl.when(i == 0)
    def _():
        carry_ref[...] = jnp.zeros_like(carry_ref)

    seq = pl.ds(pl.program_id(0), 1)
    scale = g_ref[layer:layer + 1, :] * (1.0 + mod_ref[4, seq, :])
    for r0 in range(0, tm, tm // HEAD_ROW_BLOCKS):
        rows = slice(r0, r0 + tm // HEAD_ROW_BLOCKS)
        mixed = jnp.concatenate([attn_ref[rows, :], pool_ref[rows, :]], axis=1)
        y = jnp.dot(mixed, wo_ref[...], preferred_element_type=F32)
        x1 = x_ref[rows, :] + mod_ref[2, seq, :] * y
        o_ref[rows, :] = x1
        ms = jnp.mean(x1 * x1, axis=-1, keepdims=True)
        h_ref[rows, :] = ((x1 * lax.rsqrt(ms + EPS)) * scale + mod_ref[3, seq, :]).astype(BF16)

    def conv_branch(c0):
        cols = slice(c0, c0 + FF_CHUNK)
        a = jnp.dot(h_ref[...], wup_ref[:, cols], preferred_element_type=F32)
        ext = jnp.concatenate([carry_ref[:, cols], a], axis=0)
        carry_ref[:, cols] = a[tm - SUBLANES:tm, :]
        inner = ext * cw_ref[layer, 1:2, cols] + pltpu.roll(ext * cw_ref[layer, 0:1, cols], 1, axis=0)
        return (cb_ref[layer:layer + 1, cols] + a * cw_ref[layer, 2:3, cols]
                + pltpu.roll(inner, 1, axis=0)[SUBLANES:])

    for c in range(N_FF_CHUNKS):
        gate = conv_branch(c * FF_CHUNK)
        val = conv_branch(D_FF + c * FF_CHUNK)
        half_gv = (0.5 * gate) * val
        act_ref[:, c * FF_CHUNK:(c + 1) * FF_CHUNK] = (
            half_gv + half_gv * jnp.tanh(0.5 * gate)).astype(BF16)

    y2 = jnp.dot(act_ref[...], wdn_ref[...], preferred_element_type=F32)
    x2 = o_ref[...] + mod_ref[5, seq, :] * y2
    if final:
        ms2 = jnp.mean(x2 * x2, axis=-1, keepdims=True)
        x2 = (x2 * lax.rsqrt(ms2 + EPS)) * fg_ref[...]
    o_ref[...] = x2


def _ffn(x, attn, pool, mod, g2, wo, wup, cw, cb, wdn, fg, layer, final):
    b, s, d = x.shape
    tm = TM_FFN
    kern = functools.partial(_ffn_kernel, tm=tm, layer=layer, final=final)
    row = lambda bi, i: (bi, i, 0)
    const2 = lambda bi, i: (0, 0)
    once = pl.Buffered(1)
    return pl.pallas_call(
        kern,
        grid=(b, s // tm),
        in_specs=[
            pl.BlockSpec((None, tm, d), row),
            pl.BlockSpec((None, tm, ATTN_W), row),
            pl.BlockSpec((None, tm, POOL_W), row),
            pl.BlockSpec((None, N_MOD, b, d), lambda bi, i: (layer, 0, 0, 0)),
            pl.BlockSpec(g2.shape, const2),
            pl.BlockSpec((d, d), const2, pipeline_mode=once),
            pl.BlockSpec((d, 2 * D_FF), const2, pipeline_mode=once),
            pl.BlockSpec(cw.shape, lambda bi, i: (0, 0, 0)),
            pl.BlockSpec(cb.shape, const2),
            pl.BlockSpec((D_FF, d), const2, pipeline_mode=once),
            pl.BlockSpec((1, d), const2),
        ],
        out_specs=pl.BlockSpec((None, tm, d), row),
        out_shape=jax.ShapeDtypeStruct((b, s, d), F32),
        scratch_shapes=[
            pltpu.VMEM((tm, d), BF16),
            pltpu.VMEM((tm, D_FF), BF16),
            pltpu.VMEM((SUBLANES, 2 * D_FF), F32),
        ],
        compiler_params=pltpu.CompilerParams(
            dimension_semantics=("arbitrary", "arbitrary"), vmem_limit_bytes=VMEM_LIMIT),
        name="outproj_convffn",
    )(x, attn, pool, mod, g2, wo, wup, cw, cb, wdn, fg)


def kernel(x, c, mod_w, mod_b, norm1_g, norm2_g, w_in, b_f, pool_w, pool_scale, w_out, ffn_up,
           ffn_conv_w, ffn_conv_b, ffn_down, final_g):
    b, s, d = x.shape
    depth = mod_w.shape[0]
    assert (d, s % TM_PREMIX, s % TQ_ATTN, s % TM_FFN) == (D_MODEL, 0, 0, 0)

    mod = _modulation(c, mod_w, mod_b)
    fg = final_g.reshape(1, d)
    bf = b_f.reshape(depth, N_HEADS, 1)
    w_in_t = jnp.swapaxes(w_in, 1, 2)

    for l in range(depth):
        k, fk, qt, vt, ft, pool, wo, wup, wdn = _premix(
            x, mod, norm1_g, w_in_t, l, bf, pool_w, pool_scale, w_out, ffn_up, ffn_down)
        attn = _attention(k, fk, qt, vt, ft)
        x = _ffn(x, attn, pool, mod, norm2_g, wo, wup, ffn_conv_w, ffn_conv_b, wdn, fg,
                 layer=l, final=(l == depth - 1))
    return x
```

```python
import functools

import jax
import jax.numpy as jnp
from jax import lax
from jax.experimental import pallas as pl
from jax.experimental.pallas import tpu as pltpu

F32 = jnp.float32
BF16 = jnp.bfloat16

D_MODEL = 1024
N_HEADS = 8
HEAD_DIM = 64
ATTN_W = N_HEADS * HEAD_DIM
POOL_WINDOWS = (2, 4, 8, 16)
POOL_C = 128
POOL_W = POOL_C * len(POOL_WINDOWS)
D_FF = 2816
CONV_W = 3
N_MOD = 6
MOD_K_ROWS = 256
EPS = 1e-6

LANES = 128
SUBLANES = 8
BF16_SUBLANES = 16
HALO = 16
TRI = 256
FF_CHUNK = 256
N_FF_CHUNKS = D_FF // FF_CHUNK
F_PIECES = 3
WT_F_ROWS = 16
ONES_LANE = 32
NEG_BIG = -1e30
LOG2E = 1.4426950408889634
V_ONES_ROWS = 16

TM_PREMIX = 1024
PREMIX_ROW_BLOCKS = 2
TQ_ATTN = 512
TK_ATTN = 256
ATTN_PAIRS = 1
TM_FFN = 1024
HEAD_ROW_BLOCKS = 4
V7X_VMEM_BYTES = 64 * 1024 * 1024
VMEM_LIMIT = V7X_VMEM_BYTES - 8 * 1024 * 1024


def _split3(x):
    hi = x.astype(BF16).astype(F32)
    r = x - hi
    mid = r.astype(BF16).astype(F32)
    lo = (r - mid).astype(BF16).astype(F32)
    return hi, mid, lo


def _mod_kernel(c_ref, w_ref, b_ref, o_ref):
    j = pl.program_id(1)
    c = c_ref[...]
    ca = (c * jax.nn.sigmoid(c)).astype(BF16)
    part = jnp.dot(ca, w_ref[...].astype(BF16), preferred_element_type=F32)
    d = o_ref.shape[-1]
    for v in range(o_ref.shape[0]):
        cols = slice(v * d, (v + 1) * d)

        @pl.when(j == 0)
        def _():
            o_ref[v] = part[:, cols] + b_ref[:, cols]

        @pl.when(j > 0)
        def _():
            o_ref[v] += part[:, cols]


def _modulation(c, mod_w, mod_b):
    depth, d, n = mod_w.shape
    b = c.shape[0]
    return pl.pallas_call(
        _mod_kernel,
        grid=(depth, d // MOD_K_ROWS),
        in_specs=[
            pl.BlockSpec((b, MOD_K_ROWS), lambda l, j: (0, j)),
            pl.BlockSpec((None, MOD_K_ROWS, n), lambda l, j: (l, j, 0)),
            pl.BlockSpec((None, 1, n), lambda l, j: (l, 0, 0)),
        ],
        out_specs=pl.BlockSpec((None, n // d, b, d), lambda l, j: (l, 0, 0, 0)),
        out_shape=jax.ShapeDtypeStruct((depth, n // d, b, d), F32),
        compiler_params=pltpu.CompilerParams(
            dimension_semantics=("arbitrary", "arbitrary"), vmem_limit_bytes=VMEM_LIMIT),
        name="modulation",
    )(c, mod_w, mod_b.reshape(depth, 1, n))


def _premix_kernel(x_ref, mod_ref, g_ref, w_ref, bf_ref, pw_ref, ps_ref,
                   wo_ref, wup_ref, wdn_ref,
                   k_ref, fk_ref, qt_ref, vt_ref, ft_ref, pool_ref, wo_bf_ref, wup_bf_ref, wdn_bf_ref,
                   carry_ref, ubuf_ref, wku_ref, wt_ref, tri_ref, *, tm, layer):
    i = pl.program_id(1)

    @pl.when((pl.program_id(0) == 0) & (i == 0))
    def _():
        a0, a1, a2, a3 = ATTN_W, 2 * ATTN_W, 3 * ATTN_W, 3 * ATTN_W + N_HEADS
        wku_ref[:ATTN_W, :] = w_ref[a0:a1, :].astype(BF16)
        wku_ref[ATTN_W:, :] = w_ref[a3:, :].astype(BF16)
        wt_ref[0:WT_F_ROWS, :] = jnp.concatenate(
            [w_ref[a2:a3, :], jnp.zeros((WT_F_ROWS - N_HEADS, w_ref.shape[1]), F32)], axis=0).astype(BF16)
        wt_ref[WT_F_ROWS:WT_F_ROWS + ATTN_W, :] = (
            w_ref[0:a0, :] * (HEAD_DIM ** -0.5 * LOG2E)).astype(BF16)
        wt_ref[WT_F_ROWS + ATTN_W:, :] = w_ref[a1:a2, :].astype(BF16)
        rr = lax.broadcasted_iota(jnp.int32, (TRI, TRI), 0)
        cc = lax.broadcasted_iota(jnp.int32, (TRI, TRI), 1)
        tri_ref[...] = jnp.where(rr <= cc, 1.0, 0.0).astype(BF16)

    @pl.when(i == 0)
    def _():
        carry_ref[...] = jnp.zeros_like(carry_ref)
        ubuf_ref[...] = jnp.zeros_like(ubuf_ref)

    seq = pl.ds(pl.program_id(0), 1)
    scale = g_ref[layer:layer + 1, :] * (1.0 + mod_ref[1, seq, :])
    nt = (((1,), (1,)), ((), ()))
    u_blocks, f_blocks = [], []
    for r0 in range(0, tm, tm // PREMIX_ROW_BLOCKS):
        rows = slice(r0, r0 + tm // PREMIX_ROW_BLOCKS)
        x = x_ref[rows, :]
        ms = jnp.mean(x * x, axis=-1, keepdims=True)
        hb = ((x * lax.rsqrt(ms + EPS)) * scale + mod_ref[0, seq, :]).astype(BF16)
        ku = lax.dot_general(hb, wku_ref[...], nt, preferred_element_type=F32)
        k_ref[rows, :] = ku[:, :ATTN_W].astype(BF16)
        u_blocks.append(ku[:, ATTN_W:])
        zt = lax.dot_general(wt_ref[...], hb, nt,
                             preferred_element_type=F32)
        qt_ref[:, rows] = zt[WT_F_ROWS:WT_F_ROWS + ATTN_W].astype(BF16)
        vt_ref[:, rows] = zt[WT_F_ROWS + ATTN_W:].astype(BF16)
        f_blocks.append(zt[0:N_HEADS])
    u = jnp.concatenate(u_blocks, axis=0)

    wo_bf_ref[...] = wo_ref[...].astype(BF16)
    wup_bf_ref[...] = wup_ref[...].astype(BF16)
    wdn_bf_ref[...] = wdn_ref[...].astype(BF16)

    fl = jnp.concatenate(f_blocks, axis=1) + bf_ref[...]
    logf = jnp.minimum(fl, 0.0) - jnp.log1p(jnp.exp(-jnp.abs(fl)))
    pieces = jnp.concatenate(list(_split3(logf)) + [jnp.zeros((SUBLANES, tm), F32)],
                             axis=0).astype(BF16)
    blocks, offset = [], carry_ref[:, 0:1]
    for c0 in range(0, tm, TRI):
        cs = jnp.dot(pieces[:, c0:c0 + TRI], tri_ref[...], preferred_element_type=F32)
        f_blk = (cs[0:8] + cs[8:16] + cs[16:24]) + offset
        offset = f_blk[:, TRI - 1:TRI]
        blocks.append(f_blk)
    f_cum = jnp.concatenate(blocks, axis=1)
    carry_ref[...] = jnp.broadcast_to(offset, carry_ref.shape)
    f_log2 = f_cum * LOG2E
    for pair in range(N_HEADS // 2):
        ft_ref[pair] = f_log2[2 * pair:2 * pair + 2]

    blk = jnp.concatenate(list(_split3(-f_log2))
                          + [jnp.zeros((LANES - F_PIECES * N_HEADS, tm), F32)], axis=0)
    blk_t = blk.T
    lane = lax.broadcasted_iota(jnp.int32, (tm, LANES), 1)
    blk_t = jnp.where((lane >= ONES_LANE) & (lane < ONES_LANE + F_PIECES), 1.0, blk_t)
    fk_ref[...] = blk_t.astype(BF16)

    pos = i * tm + lax.broadcasted_iota(jnp.int32, (tm, POOL_C), 0)
    for g, w in enumerate(POOL_WINDOWS):
        c0 = g * POOL_C
        ug = u[:, c0:c0 + POOL_C]
        ws = jnp.concatenate([ubuf_ref[:, c0:c0 + POOL_C], ug], axis=0)
        shift = 1
        while shift < w:
            ws = ws + pltpu.roll(ws, shift, axis=0)
            shift *= 2
        cnt = jnp.minimum(pos + 1, w).astype(F32)
        d = ws[HALO:] / cnt - ug
        pg = jnp.dot(d.astype(BF16), pw_ref[g].astype(BF16), preferred_element_type=F32)
        pool_ref[:, c0:c0 + POOL_C] = (pg * ps_ref[layer:layer + 1, c0:c0 + POOL_C]).astype(BF16)
    ubuf_ref[...] = u[tm - HALO:, :]


def _premix(x, mod, g1, w_in_t, layer, b_f, pool_w, pool_scale, w_out, ffn_up, ffn_down):
    b, s, d = x.shape
    tm = TM_PREMIX
    once = pl.Buffered(1)
    n_i = s // tm
    n_steps = b * n_i

    def cast_specs(w):
        rows, cols = w.shape[1:]
        share = 1
        while rows % (n_steps // share) or (rows // (n_steps // share)) % BF16_SUBLANES:
            share *= 2
        slab = rows // (n_steps // share)
        return (pl.BlockSpec((None, slab, cols), lambda bi, i: (layer, (bi * n_i + i) // share, 0)),
                pl.BlockSpec((slab, cols), lambda bi, i: ((bi * n_i + i) // share, 0)),
                jax.ShapeDtypeStruct((rows, cols), BF16))

    cast_in, cast_out, cast_shape = zip(*(cast_specs(w) for w in (w_out, ffn_up, ffn_down)))
    kern = functools.partial(_premix_kernel, tm=tm, layer=layer)
    row = lambda bi, i: (bi, i, 0)
    col = lambda bi, i: (bi, 0, i)
    const2 = lambda bi, i: (0, 0)
    per_layer = lambda bi, i: (layer, 0, 0)
    return pl.pallas_call(
        kern,
        grid=(b, s // tm),
        in_specs=[
            pl.BlockSpec((None, tm, d), row),
            pl.BlockSpec((None, N_MOD, b, d), lambda bi, i: (layer, 0, 0, 0)),
            pl.BlockSpec(g1.shape, const2),
            pl.BlockSpec((None, w_in_t.shape[1], d), per_layer, pipeline_mode=once),
            pl.BlockSpec((None, N_HEADS, 1), per_layer),
            pl.BlockSpec((None,) + pool_w.shape[1:], lambda bi, i: (layer, 0, 0, 0)),
            pl.BlockSpec(pool_scale.shape, const2),
            *cast_in,
        ],
        out_specs=[
            pl.BlockSpec((None, tm, ATTN_W), row),
            pl.BlockSpec((None, tm, LANES), row),
            pl.BlockSpec((None, ATTN_W, tm), col),
            pl.BlockSpec((None, ATTN_W, tm), col),
            pl.BlockSpec((None, N_HEADS // 2, 2, tm), lambda bi, i: (bi, 0, 0, i)),
            pl.BlockSpec((None, tm, POOL_W), row),
            *cast_out,
        ],
        out_shape=[
            jax.ShapeDtypeStruct((b, s, ATTN_W), BF16),
            jax.ShapeDtypeStruct((b, s, LANES), BF16),
            jax.ShapeDtypeStruct((b, ATTN_W, s), BF16),
            jax.ShapeDtypeStruct((b, ATTN_W, s), BF16),
            jax.ShapeDtypeStruct((b, N_HEADS // 2, 2, s), F32),
            jax.ShapeDtypeStruct((b, s, POOL_W), BF16),
            *cast_shape,
        ],
        scratch_shapes=[
            pltpu.VMEM((N_HEADS, LANES), F32),
            pltpu.VMEM((HALO, POOL_W), F32),
            pltpu.VMEM((ATTN_W + POOL_W, d), BF16),
            pltpu.VMEM((2 * ATTN_W + WT_F_ROWS, d), BF16),
            pltpu.VMEM((TRI, TRI), BF16),
        ],
        compiler_params=pltpu.CompilerParams(
            dimension_semantics=("arbitrary", "arbitrary"), vmem_limit_bytes=VMEM_LIMIT),
        name="premix",
    )(x, mod, g1, w_in_t, b_f, pool_w, pool_scale, w_out, ffn_up, ffn_down)


def _attn_kernel(k_ref, fk_ref, qt_ref, vt_ref, ft_ref, o_ref, s_scr, p_scr, *, tq, tk, seq, pairs):
    assert tq % tk == 0
    first_pair = pl.program_id(1) * pairs
    row = lax.broadcasted_iota(jnp.int32, (LANES, tq), 0)
    key_pos = lax.broadcasted_iota(jnp.int32, (tk, tq), 0)
    qry_pos = lax.broadcasted_iota(jnp.int32, (tk, tq), 1)
    ones_rows = jnp.ones((V_ONES_ROWS, tk), BF16)

    def scores(pr, t, q_all):
        r0 = t * tk
        k_aug = jnp.concatenate([k_ref[r0:r0 + tk, pr * LANES:(pr + 1) * LANES],
                                 fk_ref[r0:r0 + tk, :]], axis=1)
        s_scr[pr, t % 2] = jnp.dot(k_aug, q_all, preferred_element_type=F32)

    def softmax(pr, t, ms, mask):
        new_ms, alphas = [], []
        for hh in range(2):
            s = s_scr[pr, t % 2, :, hh * tq:(hh + 1) * tq]
            if mask is not None:
                s = jnp.where(mask, s, -jnp.inf)
            m_new = jnp.maximum(ms[hh], jnp.max(s, axis=0, keepdims=True))
            alphas.append(jnp.exp2(ms[hh] - m_new))
            new_ms.append(m_new)
            p_scr[pr, t % 2, :, hh * tq:(hh + 1) * tq] = jnp.exp2(s - m_new).astype(BF16)
        return new_ms, alphas

    def accumulate(pr, t, alphas, accs):
        r0 = t * tk
        out = []
        for hh in range(2):
            v0 = pr * LANES + hh * HEAD_DIM
            v_aug = jnp.concatenate([vt_ref[v0:v0 + HEAD_DIM, r0:r0 + tk], ones_rows], axis=0)
            pv = jnp.dot(v_aug, p_scr[pr, t % 2, :, hh * tq:(hh + 1) * tq],
                         preferred_element_type=F32)
            out.append(alphas[hh] * accs[hh] + pv)
        return out

    def build_q(pr, c0):
        qt2 = qt_ref[pr * LANES:(pr + 1) * LANES, c0:c0 + tq]
        q_augs = []
        for hh in range(2):
            head = 2 * (first_pair + pr) + hh
            q_h = jnp.where((row >= hh * HEAD_DIM) & (row < (hh + 1) * HEAD_DIM), qt2,
                            jnp.zeros_like(qt2))
            pieces = _split3(ft_ref[pr, hh:hh + 1, c0:c0 + tq])
            fblk = jnp.where((row < F_PIECES * N_HEADS) & (row % N_HEADS == head), 1.0, 0.0)
            for j in range(F_PIECES):
                fblk = jnp.where(row == ONES_LANE + j, pieces[j], fblk)
            q_augs.append(jnp.concatenate([q_h, fblk.astype(BF16)], axis=0))
        return jnp.concatenate(q_augs, axis=1)

    prs = range(pairs)
    for qb in range(seq // tq):
        c0 = qb * tq
        q_all = [build_q(pr, c0) for pr in prs]
        n_full = (tq // tk) * qb
        n_tiles = n_full + tq // tk
        ms = [[jnp.full((1, tq), NEG_BIG, F32) for _ in range(2)] for _ in prs]
        accs = [[jnp.zeros((HEAD_DIM + V_ONES_ROWS, tq), F32) for _ in range(2)] for _ in prs]
        pending = None
        for pr in prs:
            scores(pr, 0, q_all[pr])
        for t in range(n_tiles):
            if t + 1 < n_tiles:
                for pr in prs:
                    scores(pr, t + 1, q_all[pr])
            mask = None if t < n_full else (key_pos + (t - n_full) * tk <= qry_pos)
            alphas = [None] * pairs
            for pr in prs:
                ms[pr], alphas[pr] = softmax(pr, t, ms[pr], mask)
            if pending is not None:
                for pr in prs:
                    accs[pr] = accumulate(pr, pending[0], pending[1][pr], accs[pr])
            pending = (t, alphas)
        for pr in prs:
            accs[pr] = accumulate(pr, pending[0], pending[1][pr], accs[pr])
            out_t = jnp.concatenate(
                [a[:HEAD_DIM] / a[HEAD_DIM:HEAD_DIM + 1] for a in accs[pr]], axis=0)
            o_ref[c0:c0 + tq, pr * LANES:(pr + 1) * LANES] = out_t.T.astype(BF16)


def _attention(k, fk, qt, vt, ft):
    b, s, _ = k.shape
    pp = ATTN_PAIRS
    w = pp * LANES
    kern = functools.partial(_attn_kernel, tq=TQ_ATTN, tk=TK_ATTN, seq=s, pairs=pp)
    return pl.pallas_call(
        kern,
        grid=(b, N_HEADS // 2 // pp),
        in_specs=[
            pl.BlockSpec((None, s, w), lambda bi, p: (bi, 0, p)),
            pl.BlockSpec((None, s, LANES), lambda bi, p: (bi, 0, 0)),
            pl.BlockSpec((None, w, s), lambda bi, p: (bi, p, 0)),
            pl.BlockSpec((None, w, s), lambda bi, p: (bi, p, 0)),
            pl.BlockSpec((None, pp, 2, s), lambda bi, p: (bi, p, 0, 0)),
        ],
        out_specs=pl.BlockSpec((None, s, w), lambda bi, p: (bi, 0, p)),
        out_shape=jax.ShapeDtypeStruct((b, s, ATTN_W), BF16),
        scratch_shapes=[
            pltpu.VMEM((pp, 2, TK_ATTN, 2 * TQ_ATTN), F32),
            pltpu.VMEM((pp, 2, TK_ATTN, 2 * TQ_ATTN), BF16),
        ],
        compiler_params=pltpu.CompilerParams(
            dimension_semantics=("arbitrary", "arbitrary"), vmem_limit_bytes=VMEM_LIMIT),
        name="fox_attention",
    )(k, fk, qt, vt, ft)


def _ffn_kernel(x_ref, attn_ref, pool_ref, mod_ref, g_ref, wo_ref, wup_ref, cw_ref, cb_ref,
                wdn_ref, fg_ref, o_ref, h_ref, act_ref, carry_ref, *, tm, layer, final):
    i = pl.program_id(1)

    @pl.when(i == 0)
    def _():
        carry_ref[...] = jnp.zeros_like(carry_ref)

    seq = pl.ds(pl.program_id(0), 1)
    scale = g_ref[layer:layer + 1, :] * (1.0 + mod_ref[4, seq, :])
    for r0 in range(0, tm, tm // HEAD_ROW_BLOCKS):
        rows = slice(r0, r0 + tm // HEAD_ROW_BLOCKS)
        mixed = jnp.concatenate([attn_ref[rows, :], pool_ref[rows, :]], axis=1)
        y = jnp.dot(mixed, wo_ref[...], preferred_element_type=F32)
        x1 = x_ref[rows, :] + mod_ref[2, seq, :] * y
        o_ref[rows, :] = x1
        ms = jnp.mean(x1 * x1, axis=-1, keepdims=True)
        h_ref[rows, :] = ((x1 * lax.rsqrt(ms + EPS)) * scale + mod_ref[3, seq, :]).astype(BF16)

    def conv_branch(c0):
        cols = slice(c0, c0 + FF_CHUNK)
        a = jnp.dot(h_ref[...], wup_ref[:, cols], preferred_element_type=F32)
        ext = jnp.concatenate([carry_ref[:, cols], a], axis=0)
        carry_ref[:, cols] = a[tm - SUBLANES:tm, :]
        a1 = pltpu.roll(ext, 1, axis=0)[SUBLANES:]
        a2 = pltpu.roll(ext, 2, axis=0)[SUBLANES:]
        return (cb_ref[layer:layer + 1, cols] + a2 * cw_ref[layer, 0:1, cols]
                + a1 * cw_ref[layer, 1:2, cols] + a * cw_ref[layer, 2:3, cols])

    for c in range(N_FF_CHUNKS):
        gate = conv_branch(c * FF_CHUNK)
        val = conv_branch(D_FF + c * FF_CHUNK)
        half_gv = (0.5 * gate) * val
        act_ref[:, c * FF_CHUNK:(c + 1) * FF_CHUNK] = (
            half_gv + half_gv * jnp.tanh(0.5 * gate)).astype(BF16)

    y2 = jnp.dot(act_ref[...], wdn_ref[...], preferred_element_type=F32)
    x2 = o_ref[...] + mod_ref[5, seq, :] * y2
    if final:
        ms2 = jnp.mean(x2 * x2, axis=-1, keepdims=True)
        x2 = (x2 * lax.rsqrt(ms2 + EPS)) * fg_ref[...]
    o_ref[...] = x2


def _ffn(x, attn, pool, mod, g2, wo, wup, cw, cb, wdn, fg, layer, final):
    b, s, d = x.shape
    tm = TM_FFN
    kern = functools.partial(_ffn_kernel, tm=tm, layer=layer, final=final)
    row = lambda bi, i: (bi, i, 0)
    const2 = lambda bi, i: (0, 0)
    once = pl.Buffered(1)
    return pl.pallas_call(
        kern,
        grid=(b, s // tm),
        in_specs=[
            pl.BlockSpec((None, tm, d), row),
            pl.BlockSpec((None, tm, ATTN_W), row),
            pl.BlockSpec((None, tm, POOL_W), row),
            pl.BlockSpec((None, N_MOD, b, d), lambda bi, i: (layer, 0, 0, 0)),
            pl.BlockSpec(g2.shape, const2),
            pl.BlockSpec((d, d), const2, pipeline_mode=once),
            pl.BlockSpec((d, 2 * D_FF), const2, pipeline_mode=once),
            pl.BlockSpec(cw.shape, lambda bi, i: (0, 0, 0)),
            pl.BlockSpec(cb.shape, const2),
            pl.BlockSpec((D_FF, d), const2, pipeline_mode=once),
            pl.BlockSpec((1, d), const2),
        ],
        out_specs=pl.BlockSpec((None, tm, d), row),
        out_shape=jax.ShapeDtypeStruct((b, s, d), F32),
        scratch_shapes=[
            pltpu.VMEM((tm, d), BF16),
            pltpu.VMEM((tm, D_FF), BF16),
            pltpu.VMEM((SUBLANES, 2 * D_FF), F32),
        ],
        compiler_params=pltpu.CompilerParams(
            dimension_semantics=("arbitrary", "arbitrary"), vmem_limit_bytes=VMEM_LIMIT),
        name="outproj_convffn",
    )(x, attn, pool, mod, g2, wo, wup, cw, cb, wdn, fg)


def kernel(x, c, mod_w, mod_b, norm1_g, norm2_g, w_in, b_f, pool_w, pool_scale, w_out, ffn_up,
           ffn_conv_w, ffn_conv_b, ffn_down, final_g):
    b, s, d = x.shape
    depth = mod_w.shape[0]
    assert (d, s % TM_PREMIX, s % TQ_ATTN, s % TM_FFN) == (D_MODEL, 0, 0, 0)

    mod = _modulation(c, mod_w, mod_b)
    fg = final_g.reshape(1, d)
    bf = b_f.reshape(depth, N_HEADS, 1)
    w_in_t = jnp.swapaxes(w_in, 1, 2)

    for l in range(depth):
        k, fk, qt, vt, ft, pool, wo, wup, wdn = _premix(
            x, mod, norm1_g, w_in_t, l, bf, pool_w, pool_scale, w_out, ffn_up, ffn_down)
        attn = _attention(k, fk, qt, vt, ft)
        x = _ffn(x, attn, pool, mod, norm2_g, wo, wup, ffn_conv_w, ffn_conv_b, wdn, fg,
                 layer=l, final=(l == depth - 1))
    return x
```

```python
import functools

import jax
import jax.numpy as jnp
from jax import lax
from jax.experimental import pallas as pl
from jax.experimental.pallas import tpu as pltpu

F32 = jnp.float32
BF16 = jnp.bfloat16

D_MODEL = 1024
N_HEADS = 8
HEAD_DIM = 64
ATTN_W = N_HEADS * HEAD_DIM
POOL_WINDOWS = (2, 4, 8, 16)
POOL_C = 128
POOL_W = POOL_C * len(POOL_WINDOWS)
D_FF = 2816
CONV_W = 3
N_MOD = 6
MOD_VECS_PER_STEP = 2
EPS = 1e-6

LANES = 128
SUBLANES = 8
BF16_SUBLANES = 16
HALO = 16
TRI = 256
FF_CHUNK = 256
N_FF_CHUNKS = D_FF // FF_CHUNK
F_PIECES = 3
WT_F_ROWS = 16
ONES_LANE = 32
NEG_BIG = -1e30
LOG2E = 1.4426950408889634
V_ONES_ROWS = 16

TM_PREMIX = 1024
PREMIX_ROW_BLOCKS = 2
TQ_ATTN = 512
TK_ATTN = 256
ATTN_PAIRS = 1
TM_FFN = 1024
HEAD_ROW_BLOCKS = 4
V7X_VMEM_BYTES = 64 * 1024 * 1024
VMEM_LIMIT = V7X_VMEM_BYTES - 8 * 1024 * 1024


def _split3(x):
    hi = x.astype(BF16).astype(F32)
    r = x - hi
    mid = r.astype(BF16).astype(F32)
    lo = (r - mid).astype(BF16).astype(F32)
    return hi, mid, lo


def _mod_kernel(c_ref, w_ref, b_ref, o_ref):
    c = c_ref[...]
    ca = (c * jax.nn.sigmoid(c)).astype(BF16)
    res = jnp.dot(ca, w_ref[...].astype(BF16), preferred_element_type=F32) + b_ref[...]
    d = c.shape[1]
    for v in range(o_ref.shape[0]):
        o_ref[v] = res[:, v * d:(v + 1) * d]


def _modulation(c, mod_w, mod_b):
    depth, d, n = mod_w.shape
    b = c.shape[0]
    cols = MOD_VECS_PER_STEP * d
    return pl.pallas_call(
        _mod_kernel,
        grid=(depth, n // cols),
        in_specs=[
            pl.BlockSpec((b, d), lambda l, j: (0, 0)),
            pl.BlockSpec((None, d, cols), lambda l, j: (l, 0, j)),
            pl.BlockSpec((None, 1, cols), lambda l, j: (l, 0, j)),
        ],
        out_specs=pl.BlockSpec((None, MOD_VECS_PER_STEP, b, d), lambda l, j: (l, j, 0, 0)),
        out_shape=jax.ShapeDtypeStruct((depth, n // d, b, d), F32),
        compiler_params=pltpu.CompilerParams(
            dimension_semantics=("arbitrary", "arbitrary"), vmem_limit_bytes=VMEM_LIMIT),
        name="modulation",
    )(c, mod_w, mod_b.reshape(depth, 1, n))


def _premix_kernel(x_ref, mod_ref, g_ref, w_ref, bf_ref, pw_ref, ps_ref,
                   wo_ref, wup_ref, wdn_ref,
                   k_ref, fk_ref, qt_ref, vt_ref, ft_ref, pool_ref, wo_bf_ref, wup_bf_ref, wdn_bf_ref,
                   carry_ref, ubuf_ref, wku_ref, wt_ref, tri_ref, *, tm, layer):
    i = pl.program_id(1)

    @pl.when((pl.program_id(0) == 0) & (i == 0))
    def _():
        a0, a1, a2, a3 = ATTN_W, 2 * ATTN_W, 3 * ATTN_W, 3 * ATTN_W + N_HEADS
        wku_ref[:ATTN_W, :] = w_ref[a0:a1, :].astype(BF16)
        wku_ref[ATTN_W:, :] = w_ref[a3:, :].astype(BF16)
        wt_ref[0:WT_F_ROWS, :] = jnp.concatenate(
            [w_ref[a2:a3, :], jnp.zeros((WT_F_ROWS - N_HEADS, w_ref.shape[1]), F32)], axis=0).astype(BF16)
        wt_ref[WT_F_ROWS:WT_F_ROWS + ATTN_W, :] = (
            w_ref[0:a0, :] * (HEAD_DIM ** -0.5 * LOG2E)).astype(BF16)
        wt_ref[WT_F_ROWS + ATTN_W:, :] = w_ref[a1:a2, :].astype(BF16)
        rr = lax.broadcasted_iota(jnp.int32, (TRI, TRI), 0)
        cc = lax.broadcasted_iota(jnp.int32, (TRI, TRI), 1)
        tri_ref[...] = jnp.where(rr <= cc, 1.0, 0.0).astype(BF16)

    @pl.when(i == 0)
    def _():
        carry_ref[...] = jnp.zeros_like(carry_ref)
        ubuf_ref[...] = jnp.zeros_like(ubuf_ref)

    seq = pl.ds(pl.program_id(0), 1)
    scale = g_ref[layer:layer + 1, :] * (1.0 + mod_ref[1, seq, :])
    nt = (((1,), (1,)), ((), ()))
    u_blocks, f_blocks = [], []
    for r0 in range(0, tm, tm // PREMIX_ROW_BLOCKS):
        rows = slice(r0, r0 + tm // PREMIX_ROW_BLOCKS)
        x = x_ref[rows, :]
        ms = jnp.mean(x * x, axis=-1, keepdims=True)
        hb = ((x * lax.rsqrt(ms + EPS)) * scale + mod_ref[0, seq, :]).astype(BF16)
        ku = lax.dot_general(hb, wku_ref[...], nt, preferred_element_type=F32)
        k_ref[rows, :] = ku[:, :ATTN_W].astype(BF16)
        u_blocks.append(ku[:, ATTN_W:])
        zt = lax.dot_general(wt_ref[...], hb, nt,
                             preferred_element_type=F32)
        qt_ref[:, rows] = zt[WT_F_ROWS:WT_F_ROWS + ATTN_W].astype(BF16)
        vt_ref[:, rows] = zt[WT_F_ROWS + ATTN_W:].astype(BF16)
        f_blocks.append(zt[0:N_HEADS])
    u = jnp.concatenate(u_blocks, axis=0)

    wo_bf_ref[...] = wo_ref[...].astype(BF16)
    wup_bf_ref[...] = wup_ref[...].astype(BF16)
    wdn_bf_ref[...] = wdn_ref[...].astype(BF16)

    fl = jnp.concatenate(f_blocks, axis=1) + bf_ref[...]
    logf = jnp.minimum(fl, 0.0) - jnp.log1p(jnp.exp(-jnp.abs(fl)))
    pieces = jnp.concatenate(list(_split3(logf)) + [jnp.zeros((SUBLANES, tm), F32)],
                             axis=0).astype(BF16)
    blocks, offset = [], carry_ref[:, 0:1]
    for c0 in range(0, tm, TRI):
        cs = jnp.dot(pieces[:, c0:c0 + TRI], tri_ref[...], preferred_element_type=F32)
        f_blk = (cs[0:8] + cs[8:16] + cs[16:24]) + offset
        offset = f_blk[:, TRI - 1:TRI]
        blocks.append(f_blk)
    f_cum = jnp.concatenate(blocks, axis=1)
    carry_ref[...] = jnp.broadcast_to(offset, carry_ref.shape)
    f_log2 = f_cum * LOG2E
    for pair in range(N_HEADS // 2):
        ft_ref[pair] = f_log2[2 * pair:2 * pair + 2]

    blk = jnp.concatenate(list(_split3(-f_log2))
                          + [jnp.zeros((LANES - F_PIECES * N_HEADS, tm), F32)], axis=0)
    blk_t = blk.T
    lane = lax.broadcasted_iota(jnp.int32, (tm, LANES), 1)
    blk_t = jnp.where((lane >= ONES_LANE) & (lane < ONES_LANE + F_PIECES), 1.0, blk_t)
    fk_ref[...] = blk_t.astype(BF16)

    pos = i * tm + lax.broadcasted_iota(jnp.int32, (tm, POOL_C), 0)
    for g, w in enumerate(POOL_WINDOWS):
        c0 = g * POOL_C
        ug = u[:, c0:c0 + POOL_C]
        ws = jnp.concatenate([ubuf_ref[:, c0:c0 + POOL_C], ug], axis=0)
        shift = 1
        while shift < w:
            ws = ws + pltpu.roll(ws, shift, axis=0)
            shift *= 2
        cnt = jnp.minimum(pos + 1, w).astype(F32)
        d = ws[HALO:] / cnt - ug
        pg = jnp.dot(d.astype(BF16), pw_ref[g].astype(BF16), preferred_element_type=F32)
        pool_ref[:, c0:c0 + POOL_C] = (pg * ps_ref[layer:layer + 1, c0:c0 + POOL_C]).astype(BF16)
    ubuf_ref[...] = u[tm - HALO:, :]


def _premix(x, mod, g1, w_in_t, layer, b_f, pool_w, pool_scale, w_out, ffn_up, ffn_down):
    b, s, d = x.shape
    tm = TM_PREMIX
    once = pl.Buffered(1)
    n_i = s // tm
    n_steps = b * n_i

    def cast_specs(w):
        rows, cols = w.shape[1:]
        share = 1
        while rows % (n_steps // share) or (rows // (n_steps // share)) % BF16_SUBLANES:
            share *= 2
        slab = rows // (n_steps // share)
        return (pl.BlockSpec((None, slab, cols), lambda bi, i: (layer, (bi * n_i + i) // share, 0)),
                pl.BlockSpec((slab, cols), lambda bi, i: ((bi * n_i + i) // share, 0)),
                jax.ShapeDtypeStruct((rows, cols), BF16))

    cast_in, cast_out, cast_shape = zip(*(cast_specs(w) for w in (w_out, ffn_up, ffn_down)))
    kern = functools.partial(_premix_kernel, tm=tm, layer=layer)
    row = lambda bi, i: (bi, i, 0)
    col = lambda bi, i: (bi, 0, i)
    const2 = lambda bi, i: (0, 0)
    per_layer = lambda bi, i: (layer, 0, 0)
    return pl.pallas_call(
        kern,
        grid=(b, s // tm),
        in_specs=[
            pl.BlockSpec((None, tm, d), row),
            pl.BlockSpec((None, N_MOD, b, d), lambda bi, i: (layer, 0, 0, 0)),
            pl.BlockSpec(g1.shape, const2),
            pl.BlockSpec((None, w_in_t.shape[1], d), per_layer, pipeline_mode=once),
            pl.BlockSpec((None, N_HEADS, 1), per_layer),
            pl.BlockSpec((None,) + pool_w.shape[1:], lambda bi, i: (layer, 0, 0, 0)),
            pl.BlockSpec(pool_scale.shape, const2),
            *cast_in,
        ],
        out_specs=[
            pl.BlockSpec((None, tm, ATTN_W), row),
            pl.BlockSpec((None, tm, LANES), row),
            pl.BlockSpec((None, ATTN_W, tm), col),
            pl.BlockSpec((None, ATTN_W, tm), col),
            pl.BlockSpec((None, N_HEADS // 2, 2, tm), lambda bi, i: (bi, 0, 0, i)),
            pl.BlockSpec((None, tm, POOL_W), row),
            *cast_out,
        ],
        out_shape=[
            jax.ShapeDtypeStruct((b, s, ATTN_W), BF16),
            jax.ShapeDtypeStruct((b, s, LANES), BF16),
            jax.ShapeDtypeStruct((b, ATTN_W, s), BF16),
            jax.ShapeDtypeStruct((b, ATTN_W, s), BF16),
            jax.ShapeDtypeStruct((b, N_HEADS // 2, 2, s), F32),
            jax.ShapeDtypeStruct((b, s, POOL_W), BF16),
            *cast_shape,
        ],
        scratch_shapes=[
            pltpu.VMEM((N_HEADS, LANES), F32),
            pltpu.VMEM((HALO, POOL_W), F32),
            pltpu.VMEM((ATTN_W + POOL_W, d), BF16),
            pltpu.VMEM((2 * ATTN_W + WT_F_ROWS, d), BF16),
            pltpu.VMEM((TRI, TRI), BF16),
        ],
        compiler_params=pltpu.CompilerParams(
            dimension_semantics=("arbitrary", "arbitrary"), vmem_limit_bytes=VMEM_LIMIT),
        name="premix",
    )(x, mod, g1, w_in_t, b_f, pool_w, pool_scale, w_out, ffn_up, ffn_down)


def _attn_kernel(k_ref, fk_ref, qt_ref, vt_ref, ft_ref, o_ref, s_scr, p_scr, *, tq, tk, seq, pairs):
    assert tq % tk == 0
    first_pair = pl.program_id(1) * pairs
    row = lax.broadcasted_iota(jnp.int32, (LANES, tq), 0)
    key_pos = lax.broadcasted_iota(jnp.int32, (tk, tq), 0)
    qry_pos = lax.broadcasted_iota(jnp.int32, (tk, tq), 1)
    ones_rows = jnp.ones((V_ONES_ROWS, tk), BF16)

    def scores(pr, t, q_all):
        r0 = t * tk
        k_aug = jnp.concatenate([k_ref[r0:r0 + tk, pr * LANES:(pr + 1) * LANES],
                                 fk_ref[r0:r0 + tk, :]], axis=1)
        s_scr[pr, t % 2] = jnp.dot(k_aug, q_all, preferred_element_type=F32)

    def softmax(pr, t, ms, mask):
        new_ms, alphas = [], []
        for hh in range(2):
            s = s_scr[pr, t % 2, :, hh * tq:(hh + 1) * tq]
            if mask is not None:
                s = jnp.where(mask, s, -jnp.inf)
            m_new = jnp.maximum(ms[hh], jnp.max(s, axis=0, keepdims=True))
            alphas.append(jnp.exp2(ms[hh] - m_new))
            new_ms.append(m_new)
            p_scr[pr, t % 2, :, hh * tq:(hh + 1) * tq] = jnp.exp2(s - m_new).astype(BF16)
        return new_ms, alphas

    def accumulate(pr, t, alphas, accs):
        r0 = t * tk
        out = []
        for hh in range(2):
            v0 = pr * LANES + hh * HEAD_DIM
            v_aug = jnp.concatenate([vt_ref[v0:v0 + HEAD_DIM, r0:r0 + tk], ones_rows], axis=0)
            pv = jnp.dot(v_aug, p_scr[pr, t % 2, :, hh * tq:(hh + 1) * tq],
                         preferred_element_type=F32)
            out.append(alphas[hh] * accs[hh] + pv)
        return out

    def build_q(pr, c0):
        qt2 = qt_ref[pr * LANES:(pr + 1) * LANES, c0:c0 + tq]
        q_augs = []
        for hh in range(2):
            head = 2 * (first_pair + pr) + hh
            q_h = jnp.where((row >= hh * HEAD_DIM) & (row < (hh + 1) * HEAD_DIM), qt2,
                            jnp.zeros_like(qt2))
            pieces = _split3(ft_ref[pr, hh:hh + 1, c0:c0 + tq])
            fblk = jnp.where((row < F_PIECES * N_HEADS) & (row % N_HEADS == head), 1.0, 0.0)
            for j in range(F_PIECES):
                fblk = jnp.where(row == ONES_LANE + j, pieces[j], fblk)
            q_augs.append(jnp.concatenate([q_h, fblk.astype(BF16)], axis=0))
        return jnp.concatenate(q_augs, axis=1)

    assert pairs == 1
    for qb in range(seq // tq):
        c0 = qb * tq
        q_all = build_q(0, c0)
        n_full = (tq // tk) * qb
        n_tiles = n_full + tq // tk
        s_tiles = []
        for t in range(n_tiles):
            r0 = t * tk
            k_aug = jnp.concatenate([k_ref[r0:r0 + tk, 0:LANES], fk_ref[r0:r0 + tk, :]], axis=1)
            s = jnp.dot(k_aug, q_all, preferred_element_type=F32)
            if t >= n_full:
                mask = key_pos + (t - n_full) * tk <= qry_pos
                s = jnp.where(jnp.concatenate([mask, mask], axis=1), s, -jnp.inf)
            s_tiles.append(s)
        m = functools.reduce(jnp.maximum, [jnp.max(s, axis=0, keepdims=True) for s in s_tiles])
        accs = [None, None]
        for t in range(n_tiles):
            r0 = t * tk
            pr_t = jnp.exp2(s_tiles[t] - m).astype(BF16)
            for hh in range(2):
                v0 = hh * HEAD_DIM
                v_aug = jnp.concatenate([vt_ref[v0:v0 + HEAD_DIM, r0:r0 + tk], ones_rows], axis=0)
                pv = jnp.dot(v_aug, pr_t[:, hh * tq:(hh + 1) * tq], preferred_element_type=F32)
                accs[hh] = pv if accs[hh] is None else accs[hh] + pv
        out_t = jnp.concatenate(
            [a[:HEAD_DIM] / a[HEAD_DIM:HEAD_DIM + 1] for a in accs], axis=0)
        o_ref[c0:c0 + tq, 0:LANES] = out_t.T.astype(BF16)


def _attention(k, fk, qt, vt, ft):
    b, s, _ = k.shape
    pp = ATTN_PAIRS
    w = pp * LANES
    kern = functools.partial(_attn_kernel, tq=TQ_ATTN, tk=TK_ATTN, seq=s, pairs=pp)
    return pl.pallas_call(
        kern,
        grid=(b, N_HEADS // 2 // pp),
        in_specs=[
            pl.BlockSpec((None, s, w), lambda bi, p: (bi, 0, p)),
            pl.BlockSpec((None, s, LANES), lambda bi, p: (bi, 0, 0)),
            pl.BlockSpec((None, w, s), lambda bi, p: (bi, p, 0)),
            pl.BlockSpec((None, w, s), lambda bi, p: (bi, p, 0)),
            pl.BlockSpec((None, pp, 2, s), lambda bi, p: (bi, p, 0, 0)),
        ],
        out_specs=pl.BlockSpec((None, s, w), lambda bi, p: (bi, 0, p)),
        out_shape=jax.ShapeDtypeStruct((b, s, ATTN_W), BF16),
        scratch_shapes=[
            pltpu.VMEM((pp, 2, TK_ATTN, 2 * TQ_ATTN), F32),
            pltpu.VMEM((pp, 2, TK_ATTN, 2 * TQ_ATTN), BF16),
        ],
        compiler_params=pltpu.CompilerParams(
            dimension_semantics=("arbitrary", "arbitrary"), vmem_limit_bytes=VMEM_LIMIT),
        name="fox_attention",
    )(k, fk, qt, vt, ft)


def _ffn_kernel(x_ref, attn_ref, pool_ref, mod_ref, g_ref, wo_ref, wup_ref, cw_ref, cb_ref,
                wdn_ref, fg_ref, o_ref, h_ref, act_ref, carry_ref, *, tm, layer, final):
    i = pl.program_id(1)

    @pl.when(i == 0)
    def _():
        carry_ref[...] = jnp.zeros_like(carry_ref)

    seq = pl.ds(pl.program_id(0), 1)
    scale = g_ref[layer:layer + 1, :] * (1.0 + mod_ref[4, seq, :])
    for r0 in range(0, tm, tm // HEAD_ROW_BLOCKS):
        rows = slice(r0, r0 + tm // HEAD_ROW_BLOCKS)
        mixed = jnp.concatenate([attn_ref[rows, :], pool_ref[rows, :]], axis=1)
        y = jnp.dot(mixed, wo_ref[...], preferred_element_type=F32)
        x1 = x_ref[rows, :] + mod_ref[2, seq, :] * y
        o_ref[rows, :] = x1
        ms = jnp.mean(x1 * x1, axis=-1, keepdims=True)
        h_ref[rows, :] = ((x1 * lax.rsqrt(ms + EPS)) * scale + mod_ref[3, seq, :]).astype(BF16)

    def conv_branch(c0):
        cols = slice(c0, c0 + FF_CHUNK)
        a = jnp.dot(h_ref[...], wup_ref[:, cols], preferred_element_type=F32)
        ext = jnp.concatenate([carry_ref[:, cols], a], axis=0)
        carry_ref[:, cols] = a[tm - SUBLANES:tm, :]
        a1 = pltpu.roll(ext, 1, axis=0)[SUBLANES:]
        a2 = pltpu.roll(ext, 2, axis=0)[SUBLANES:]
        return (cb_ref[layer:layer + 1, cols] + a2 * cw_ref[layer, 0:1, cols]
                + a1 * cw_ref[layer, 1:2, cols] + a * cw_ref[layer, 2:3, cols])

    for c in range(N_FF_CHUNKS):
        gate = conv_branch(c * FF_CHUNK)
        val = conv_branch(D_FF + c * FF_CHUNK)
        half_gv = (0.5 * gate) * val
        act_ref[:, c * FF_CHUNK:(c + 1) * FF_CHUNK] = (
            half_gv + half_gv * jnp.tanh(0.5 * gate)).astype(BF16)

    y2 = jnp.dot(act_ref[...], wdn_ref[...], preferred_element_type=F32)
    x2 = o_ref[...] + mod_ref[5, seq, :] * y2
    if final:
        ms2 = jnp.mean(x2 * x2, axis=-1, keepdims=True)
        x2 = (x2 * lax.rsqrt(ms2 + EPS)) * fg_ref[...]
    o_ref[...] = x2


def _ffn(x, attn, pool, mod, g2, wo, wup, cw, cb, wdn, fg, layer, final):
    b, s, d = x.shape
    tm = TM_FFN
    kern = functools.partial(_ffn_kernel, tm=tm, layer=layer, final=final)
    row = lambda bi, i: (bi, i, 0)
    const2 = lambda bi, i: (0, 0)
    once = pl.Buffered(1)
    return pl.pallas_call(
        kern,
        grid=(b, s // tm),
        in_specs=[
            pl.BlockSpec((None, tm, d), row),
            pl.BlockSpec((None, tm, ATTN_W), row),
            pl.BlockSpec((None, tm, POOL_W), row),
            pl.BlockSpec((None, N_MOD, b, d), lambda bi, i: (layer, 0, 0, 0)),
            pl.BlockSpec(g2.shape, const2),
            pl.BlockSpec((d, d), const2, pipeline_mode=once),
            pl.BlockSpec((d, 2 * D_FF), const2, pipeline_mode=once),
            pl.BlockSpec(cw.shape, lambda bi, i: (0, 0, 0)),
            pl.BlockSpec(cb.shape, const2),
            pl.BlockSpec((D_FF, d), const2, pipeline_mode=once),
            pl.BlockSpec((1, d), const2),
        ],
        out_specs=pl.BlockSpec((None, tm, d), row),
        out_shape=jax.ShapeDtypeStruct((b, s, d), F32),
        scratch_shapes=[
            pltpu.VMEM((tm, d), BF16),
            pltpu.VMEM((tm, D_FF), BF16),
            pltpu.VMEM((SUBLANES, 2 * D_FF), F32),
        ],
        compiler_params=pltpu.CompilerParams(
            dimension_semantics=("arbitrary", "arbitrary"), vmem_limit_bytes=VMEM_LIMIT),
        name="outproj_convffn",
    )(x, attn, pool, mod, g2, wo, wup, cw, cb, wdn, fg)


def kernel(x, c, mod_w, mod_b, norm1_g, norm2_g, w_in, b_f, pool_w, pool_scale, w_out, ffn_up,
           ffn_conv_w, ffn_conv_b, ffn_down, final_g):
    b, s, d = x.shape
    depth = mod_w.shape[0]
    assert (d, s % TM_PREMIX, s % TQ_ATTN, s % TM_FFN) == (D_MODEL, 0, 0, 0)

    mod = _modulation(c, mod_w, mod_b)
    fg = final_g.reshape(1, d)
    bf = b_f.reshape(depth, N_HEADS, 1)
    w_in_t = jnp.swapaxes(w_in, 1, 2)

    for l in range(depth):
        k, fk, qt, vt, ft, pool, wo, wup, wdn = _premix(
            x, mod, norm1_g, w_in_t, l, bf, pool_w, pool_scale, w_out, ffn_up, ffn_down)
        attn = _attention(k, fk, qt, vt, ft)
        x = _ffn(x, attn, pool, mod, norm2_g, wo, wup, ffn_conv_w, ffn_conv_b, wdn, fg,
                 layer=l, final=(l == depth - 1))
    return x
```
